```python
import jax, jax.numpy as jnp
from jax import lax
import numpy as np

D_MODEL = 1024
BATCH = 2
SEQ = 8192
DEPTH = 1
DEC_BATCH = 128
DEC_SEQ = 4
PAST_LEN = 2048
PAGE_SIZE = 128

N_HEADS = 8
HEAD_DIM = 64
ATT_DIM = N_HEADS * HEAD_DIM
CONV_DIM = 512
CONV_W = 3
N_EXPERTS = 64
TOP_K = 8
EXPERT_DIM = 256
SHARED_DIM = 256
ROUTED_SCALE = 2.5
Q_BLOCK = 128
EPS = 1e-6
SB_BIAS_INIT = -6.0
IN_COLS = 3 * ATT_DIM + 3 * CONV_DIM + 2 * D_MODEL
IN_SPLITS = (ATT_DIM, 2 * ATT_DIM, 3 * ATT_DIM, 3 * ATT_DIM + CONV_DIM,
             3 * ATT_DIM + 2 * CONV_DIM, 3 * ATT_DIM + 3 * CONV_DIM,
             3 * ATT_DIM + 3 * CONV_DIM + D_MODEL)

kernel_name = 'stick_breaking_shortconv_moe_step'


def rmsnorm(x, g):
    xf = x.astype(jnp.float32)
    xf = xf * lax.rsqrt(jnp.mean(xf * xf, axis=-1, keepdims=True) + EPS)
    return xf.astype(x.dtype) * g


def stick_breaking(q, k, v, q_pos, k_pos, sb_bias):
    z = jnp.einsum('bqhd,bkhd->bhqk', q.astype(jnp.float32), k.astype(jnp.float32)) * (HEAD_DIM ** -0.5)
    z = z + sb_bias.astype(jnp.float32)[None, :, None, None]
    causal = k_pos[None, :] < q_pos[:, None]
    log_keep = jnp.where(causal, jax.nn.log_sigmoid(-z), 0.0)
    log_keep_after = lax.cumsum(log_keep, axis=3, reverse=True) - log_keep
    a = jnp.where(causal, jnp.exp(jax.nn.log_sigmoid(z) + log_keep_after), 0.0)
    return jnp.einsum('bhqk,bkhd->bqhd', a.astype(v.dtype), v)


def sb_prompt(q, k, v, sb_bias):
    b, t, h, dh = q.shape
    n_blk = t // Q_BLOCK
    k_pos = jnp.arange(t, dtype=jnp.int32)
    q_blocks = jnp.moveaxis(q.reshape(b, n_blk, Q_BLOCK, h, dh), 1, 0)

    def one_block(args):
        i, q_i = args
        q_pos = i * Q_BLOCK + jnp.arange(Q_BLOCK, dtype=jnp.int32)
        return stick_breaking(q_i, k, v, q_pos, k_pos, sb_bias)

    out = lax.map(one_block, (jnp.arange(n_blk, dtype=jnp.int32), q_blocks))
    return jnp.moveaxis(out, 0, 1).reshape(b, t, h * dh)


def causal_conv(u_ext, conv_w):
    t = u_ext.shape[1] - (CONV_W - 1)
    y = conv_w[0] * u_ext[:, 0:t]
    for i in range(1, CONV_W):
        y = y + conv_w[i] * u_ext[:, i:i + t]
    return y


def mixer_inputs(xn, w_in):
    b, t, _ = xn.shape
    q, k, v, gate_b, gate_c, h, g_att, g_conv = jnp.split(xn @ w_in, IN_SPLITS, axis=-1)
    heads = lambda a: a.reshape(b, t, N_HEADS, HEAD_DIM)
    return heads(q), heads(k), heads(v), gate_b, gate_c * h, g_att, g_conv


def merge_branches(x, att, conv_out, g_att, g_conv, w_branch_att, w_branch_conv, w_out):
    merged = (jax.nn.sigmoid(g_att) * (att @ w_branch_att)
              + jax.nn.sigmoid(g_conv) * (conv_out @ w_branch_conv))
    return x + merged @ w_out


def swiglu(x, wg, wu, wd):
    return (jax.nn.silu(x @ wg) * (x @ wu)) @ wd


def moe(x, w_router, router_bias, w_gate_e, w_up_e, w_down_e, w_gate_s, w_up_s, w_down_s):
    shape = x.shape
    xt = x.reshape(-1, D_MODEL)
    n_tok = xt.shape[0]
    scores = jax.nn.sigmoid((xt @ w_router).astype(jnp.float32))
    _, idx = lax.top_k(scores + router_bias.astype(jnp.float32), TOP_K)
    sel = jnp.take_along_axis(scores, idx, axis=-1)
    wts = sel / jnp.sum(sel, axis=-1, keepdims=True) * ROUTED_SCALE
    gate = jnp.zeros((n_tok, N_EXPERTS), jnp.float32).at[jnp.arange(n_tok)[:, None], idx].set(wts)
    gate = gate.astype(x.dtype)
    y = swiglu(xt, w_gate_s, w_up_s, w_down_s)
    for e in range(N_EXPERTS):
        y = y + gate[:, e:e + 1] * swiglu(xt, w_gate_e[e], w_up_e[e], w_down_e[e])
    return y.reshape(shape)


def setup_inputs(seed: int = 0) -> dict:
    key = jax.random.key(seed)
    ks = jax.random.split(key, 24)
    n_pages = PAST_LEN // PAGE_SIZE
    n_used = DEC_BATCH * n_pages
    n_pool = n_used + n_used // 4
    nrm = lambda k, shape, scale: jax.random.normal(k, shape, jnp.float32) * scale
    page_table = jax.random.permutation(ks[0], n_pool)[:n_used].reshape(DEC_BATCH, n_pages).astype(jnp.int32)
    return {
        'x_prompt': nrm(ks[1], (BATCH, SEQ, D_MODEL), 1.0),
        'x_sample': nrm(ks[2], (DEC_BATCH, DEC_SEQ, D_MODEL), 1.0),
        'cache_k': nrm(ks[3], (DEPTH, n_pool, PAGE_SIZE, N_HEADS, HEAD_DIM), 1.0),
        'cache_v': nrm(ks[4], (DEPTH, n_pool, PAGE_SIZE, N_HEADS, HEAD_DIM), 1.0),
        'state_conv': nrm(ks[5], (DEPTH, DEC_BATCH, CONV_W - 1, CONV_DIM), 1.0),
        'page_table': page_table,
        'norm_mix_g': 1.0 + nrm(ks[6], (DEPTH, D_MODEL), 0.02),
        'w_in': nrm(ks[7], (DEPTH, D_MODEL, IN_COLS), D_MODEL ** -0.5),
        'sb_bias': SB_BIAS_INIT + nrm(ks[22], (DEPTH, N_HEADS), 0.5),
        'conv_w': nrm(ks[8], (DEPTH, CONV_W, CONV_DIM), CONV_W ** -0.5),
        'w_branch_att': nrm(ks[9], (DEPTH, ATT_DIM, D_MODEL), ATT_DIM ** -0.5),
        'w_branch_conv': nrm(ks[10], (DEPTH, CONV_DIM, D_MODEL), CONV_DIM ** -0.5),
        'w_out': nrm(ks[11], (DEPTH, D_MODEL, D_MODEL), D_MODEL ** -0.5),
        'norm_ffn_g': 1.0 + nrm(ks[12], (DEPTH, D_MODEL), 0.02),
        'w_router': nrm(ks[13], (DEPTH, D_MODEL, N_EXPERTS), D_MODEL ** -0.5),
        'router_bias': nrm(ks[14], (DEPTH, N_EXPERTS), 0.01),
        'w_gate_e': nrm(ks[15], (DEPTH, N_EXPERTS, D_MODEL, EXPERT_DIM), D_MODEL ** -0.5),
        'w_up_e': nrm(ks[16], (DEPTH, N_EXPERTS, D_MODEL, EXPERT_DIM), D_MODEL ** -0.5),
        'w_down_e': nrm(ks[17], (DEPTH, N_EXPERTS, EXPERT_DIM, D_MODEL), EXPERT_DIM ** -0.5),
        'w_gate_s': nrm(ks[18], (DEPTH, D_MODEL, SHARED_DIM), D_MODEL ** -0.5),
        'w_up_s': nrm(ks[19], (DEPTH, D_MODEL, SHARED_DIM), D_MODEL ** -0.5),
        'w_down_s': nrm(ks[20], (DEPTH, SHARED_DIM, D_MODEL), SHARED_DIM ** -0.5),
        'norm_final_g': 1.0 + nrm(ks[21], (D_MODEL,), 0.02),
    }


def reference(x_prompt, x_sample, cache_k, cache_v, state_conv, page_table,
              norm_mix_g, w_in, sb_bias, conv_w, w_branch_att, w_branch_conv, w_out,
              norm_ffn_g, w_router, router_bias, w_gate_e, w_up_e, w_down_e,
              w_gate_s, w_up_s, w_down_s, norm_final_g):
    xp, xs = x_prompt, x_sample
    b, t, _ = xp.shape
    db, ts, _ = xs.shape
    past = page_table.shape[1] * cache_k.shape[2]
    kp_rows, vp_rows, cp_rows, ks_rows, vs_rows, cs_rows = [], [], [], [], [], []
    for l in range(DEPTH):
        q, k, v, gate_b, u, g_att, g_conv = mixer_inputs(rmsnorm(xp, norm_mix_g[l]), w_in[l])
        att = sb_prompt(q, k, v, sb_bias[l])
        u_ext = jnp.concatenate([jnp.zeros((b, CONV_W - 1, CONV_DIM), u.dtype), u], axis=1)
        conv_out = gate_b * causal_conv(u_ext, conv_w[l])
        xp = merge_branches(xp, att, conv_out, g_att, g_conv, w_branch_att[l], w_branch_conv[l], w_out[l])
        kp_rows.append(k)
        vp_rows.append(v)
        cp_rows.append(u_ext[:, -(CONV_W - 1):])

        q_s, k_s, v_s, gate_b_s, u_s, g_att_s, g_conv_s = mixer_inputs(rmsnorm(xs, norm_mix_g[l]), w_in[l])
        k_all = jnp.concatenate([cache_k[l][page_table].reshape(db, past, N_HEADS, HEAD_DIM), k_s], axis=1)
        v_all = jnp.concatenate([cache_v[l][page_table].reshape(db, past, N_HEADS, HEAD_DIM), v_s], axis=1)
        q_pos = past + jnp.arange(ts, dtype=jnp.int32)
        k_pos = jnp.arange(past + ts, dtype=jnp.int32)
        att_s = stick_breaking(q_s, k_all, v_all, q_pos, k_pos, sb_bias[l]).reshape(db, ts, ATT_DIM)
        u_ext_s = jnp.concatenate([state_conv[l].astype(u_s.dtype), u_s], axis=1)
        conv_out_s = gate_b_s * causal_conv(u_ext_s, conv_w[l])
        xs = merge_branches(xs, att_s, conv_out_s, g_att_s, g_conv_s, w_branch_att[l], w_branch_conv[l], w_out[l])
        ks_rows.append(k_s)
        vs_rows.append(v_s)
        cs_rows.append(u_ext_s[:, -(CONV_W - 1):])

        moe_w = (w_router[l], router_bias[l], w_gate_e[l], w_up_e[l], w_down_e[l],
                 w_gate_s[l], w_up_s[l], w_down_s[l])
        xp = xp + moe(rmsnorm(xp, norm_ffn_g[l]), *moe_w)
        xs = xs + moe(rmsnorm(xs, norm_ffn_g[l]), *moe_w)

    y_prompt = rmsnorm(xp, norm_final_g)
    y_sample = rmsnorm(xs, norm_final_g)
    k_prompt = jnp.stack(kp_rows)
    v_prompt = jnp.stack(vp_rows)
    conv_prompt = jnp.stack(cp_rows)
    k_sample = jnp.stack(ks_rows)
    v_sample = jnp.stack(vs_rows)
    conv_sample = jnp.stack(cs_rows)
    return (y_prompt, y_sample, k_prompt, v_prompt, conv_prompt, k_sample, v_sample, conv_sample)
```

```python
import functools

import jax
import jax.numpy as jnp
from jax import lax
from jax.experimental import pallas as pl
from jax.experimental.pallas import tpu as pltpu

F32 = jnp.float32
BF16 = jnp.bfloat16

N_HEADS = 8
HEAD_DIM = 64
ATT_DIM = N_HEADS * HEAD_DIM
CONV_DIM = 512
TOP_K = 8
ROUTED_SCALE = 2.5
EPS = 1e-6

VMEM_LIMIT_BYTES = 56 * 1024 * 1024
LANES = 128
HEAD_PAIR = LANES

TM_PROJ = 512
TM_MOE = 1024
TQ = 128
KB = 256
Q_PAD = 16
PAGES_PER_STEP = 4


def _params(sem):
    return pltpu.CompilerParams(dimension_semantics=sem, vmem_limit_bytes=VMEM_LIMIT_BYTES)


def _rms(x, g):
    return (x * lax.rsqrt(jnp.mean(x * x, axis=-1, keepdims=True) + EPS)) * g


def _sigmoid(x):
    return 1.0 / (1.0 + jnp.exp(-x))


def _softplus(z):
    return jnp.maximum(z, 0.0) + jnp.log(1.0 + jnp.exp(-jnp.abs(z)))


def _dot(a, b):
    return jnp.dot(a, b, preferred_element_type=F32)


def _dot_nt(a, b):
    return lax.dot_general(a, b, (((1,), (1,)), ((), ())), preferred_element_type=F32)


def _split_bf16(x):
    hi = x.astype(BF16)
    lo = (x - hi.astype(F32)).astype(BF16)
    return hi, lo


def _tri(n):
    j = lax.broadcasted_iota(jnp.int32, (n, n), 0)
    s = lax.broadcasted_iota(jnp.int32, (n, n), 1)
    t = (j >= s).astype(BF16)
    return jnp.concatenate([t, t], axis=0)


def _project(xb, w_ref, lo, hi):
    return _dot(xb, w_ref[:, lo:hi])


def _conv_branch(xb, w_ref, cw_ref, wbc_ref, u, um1, um2):
    a = ATT_DIM
    cw = cw_ref[...]
    conv = _project(xb, w_ref, 3 * a, 3 * a + CONV_DIM) * (cw[0:1] * um2 + cw[1:2] * um1 + cw[2:3] * u)
    d = w_ref.shape[0]
    g_conv = _project(xb, w_ref, 3 * a + 3 * CONV_DIM + d, 3 * a + 3 * CONV_DIM + 2 * d)
    return _sigmoid(g_conv) * _dot(conv.astype(BF16), wbc_ref[...])


def _inproj_prompt_body(x_ref, g_ref, w_ref, cw_ref, wbc_ref,
                        q_ref, kb_ref, vb_ref, kf_ref, vf_ref, ga_ref, cb_ref, cs_ref, hist_ref):
    a, c, d = ATT_DIM, CONV_DIM, x_ref.shape[2]
    tm = x_ref.shape[1]
    xb = _rms(x_ref[0], g_ref[...]).astype(BF16)
    q_ref[0] = (_project(xb, w_ref, 0, a) * (HEAD_DIM ** -0.5)).astype(BF16)
    k = _project(xb, w_ref, a, 2 * a)
    kf_ref[0] = k
    kb_ref[0] = k.astype(BF16)
    v = _project(xb, w_ref, 2 * a, 3 * a)
    vf_ref[0] = v
    vb_ref[0] = v.astype(BF16)
    u = _project(xb, w_ref, 3 * a + c, 3 * a + 2 * c) * _project(xb, w_ref, 3 * a + 2 * c, 3 * a + 3 * c)

    @pl.when(pl.program_id(1) == 0)
    def _():
        hist_ref[...] = jnp.zeros_like(hist_ref)

    row = lax.broadcasted_iota(jnp.int32, u.shape, 0)
    h0 = hist_ref[6:7, :]
    h1 = hist_ref[7:8, :]
    um1 = jnp.where(row < 1, h1, pltpu.roll(u, 1, 0))
    um2 = jnp.where(row < 1, h0, jnp.where(row < 2, h1, pltpu.roll(u, 2, 0)))
    cb_ref[0] = _conv_branch(xb, w_ref, cw_ref, wbc_ref, u, um1, um2).astype(BF16)
    tail = u[tm - 8:tm, :]
    hist_ref[...] = tail
    cs_ref[0] = tail
    ga_ref[0] = _sigmoid(_project(xb, w_ref, 3 * a + 3 * c, 3 * a + 3 * c + d)).astype(BF16)


def _inproj_prompt(x, g, w_bf, conv_w, wbc_bf):
    b, t, d = x.shape
    tm = min(TM_PROJ, t)
    nt = t // tm
    n_in = w_bf.shape[1]
    tok = lambda n: pl.BlockSpec((1, tm, n), lambda i, j: (i, j, 0))
    full = lambda shp: pl.BlockSpec(shp, lambda i, j: (0,) * len(shp))
    out_shape = (
        jax.ShapeDtypeStruct((b, t, ATT_DIM), BF16),
        jax.ShapeDtypeStruct((b, t, ATT_DIM), BF16),
        jax.ShapeDtypeStruct((b, t, ATT_DIM), BF16),
        jax.ShapeDtypeStruct((b, t, ATT_DIM), F32),
        jax.ShapeDtypeStruct((b, t, ATT_DIM), F32),
        jax.ShapeDtypeStruct((b, t, d), BF16),
        jax.ShapeDtypeStruct((b, t, d), BF16),
        jax.ShapeDtypeStruct((b, 8, CONV_DIM), F32),
    )
    return pl.pallas_call(
        _inproj_prompt_body,
        grid=(b, nt),
        in_specs=[tok(d), full((1, d)), full((d, n_in)), full((3, CONV_DIM)), full((CONV_DIM, d))],
        out_specs=(tok(ATT_DIM), tok(ATT_DIM), tok(ATT_DIM), tok(ATT_DIM), tok(ATT_DIM), tok(d), tok(d),
                   pl.BlockSpec((1, 8, CONV_DIM), lambda i, j: (i, 0, 0))),
        out_shape=out_shape,
        scratch_shapes=[pltpu.VMEM((8, CONV_DIM), F32)],
        compiler_params=_params(("arbitrary", "arbitrary")),
        name="inproj_prompt",
    )(x, g, w_bf, conv_w, wbc_bf)


def _inproj_sample_body(ts, x_ref, g_ref, w_ref, cw_ref, wbc_ref, h1_ref, h2_ref,
                        q_ref, k_ref, v_ref, u_ref, ga_ref, cb_ref):
    a, c, d = ATT_DIM, CONV_DIM, x_ref.shape[1]
    xb = _rms(x_ref[...], g_ref[...]).astype(BF16)
    q_ref[...] = _project(xb, w_ref, 0, a) * (HEAD_DIM ** -0.5)
    k_ref[...] = _project(xb, w_ref, a, 2 * a)
    v_ref[...] = _project(xb, w_ref, 2 * a, 3 * a)
    u = _project(xb, w_ref, 3 * a + c, 3 * a + 2 * c) * _project(xb, w_ref, 3 * a + 2 * c, 3 * a + 3 * c)
    u_ref[...] = u
    tok = lax.rem(lax.broadcasted_iota(jnp.int32, u.shape, 0), ts)
    um1 = jnp.where(tok >= 1, pltpu.roll(u, 1, 0), h1_ref[...])
    um2 = jnp.where(tok >= 2, pltpu.roll(u, 2, 0), h2_ref[...])
    cb_ref[...] = _conv_branch(xb, w_ref, cw_ref, wbc_ref, u, um1, um2).astype(BF16)
    ga_ref[...] = _sigmoid(_project(xb, w_ref, 3 * a + 3 * c, 3 * a + 3 * c + d)).astype(BF16)


def _inproj_sample(x, g, w_bf, conv_w, wbc_bf, hist1, hist2, ts):
    n, d = x.shape
    out_shape = (
        jax.ShapeDtypeStruct((n, ATT_DIM), F32),
        jax.ShapeDtypeStruct((n, ATT_DIM), F32),
        jax.ShapeDtypeStruct((n, ATT_DIM), F32),
        jax.ShapeDtypeStruct((n, CONV_DIM), F32),
        jax.ShapeDtypeStruct((n, d), BF16),
        jax.ShapeDtypeStruct((n, d), BF16),
    )
    return pl.pallas_call(
        functools.partial(_inproj_sample_body, ts),
        out_shape=out_shape,
        compiler_params=pltpu.CompilerParams(vmem_limit_bytes=VMEM_LIMIT_BYTES),
        name="inproj_sample",
    )(x, g, w_bf, conv_w, wbc_bf, hist1, hist2)


def _attn_prompt_body(bias_ref, q_ref, k_ref, v_ref, t_ref, o_ref, acc_ref, carry_ref):
    p = pl.program_id(1)
    i = pl.program_id(2)
    tq = q_ref.shape[1]
    q = q_ref[0]
    lane = lax.broadcasted_iota(jnp.int32, q.shape, 1)
    zero = jnp.zeros_like(q)
    q2 = jnp.concatenate([jnp.where(lane < HEAD_DIM, q, zero), jnp.where(lane >= HEAD_DIM, q, zero)], axis=0)
    b_even = bias_ref[2 * p]
    b_odd = bias_ref[2 * p + 1]
    tri = t_ref[...]

    def block(kstart, masked):
        kblk = k_ref[0, pl.ds(kstart, KB), :]
        vblk = v_ref[0, pl.ds(kstart, KB), :]
        s = _dot_nt(q2, kblk)
        z = jnp.concatenate([s[:tq] + b_even, s[tq:] + b_odd], axis=0)
        sp = _softplus(z)
        if masked:
            r = lax.broadcasted_iota(jnp.int32, (tq, KB), 0)
            col = lax.broadcasted_iota(jnp.int32, (tq, KB), 1)
            m1 = (kstart + col) < (i * tq + r)
            mask = jnp.concatenate([m1, m1], axis=0)
            sp = jnp.where(mask, sp, 0.0)
        hi, lo = _split_bf16(sp)
        carry = carry_ref[...]
        c = _dot(jnp.concatenate([hi, lo], axis=1), tri) + jnp.concatenate([carry, carry], axis=1)
        a = jnp.exp(z - c)
        if masked:
            a = jnp.where(mask, a, 0.0)
        acc_ref[...] += _dot(a.astype(BF16), vblk)
        carry_ref[...] = jnp.broadcast_to(c[:, 0:1], carry_ref.shape)

    acc_ref[...] = jnp.zeros_like(acc_ref)
    carry_ref[...] = jnp.zeros_like(carry_ref)
    jd = (i * tq) // KB
    block(pl.multiple_of(jd * KB, KB), True)

    def body(it, _):
        block(pl.multiple_of((jd - 1 - it) * KB, KB), False)
        return 0

    lax.fori_loop(0, jd, body, 0)
    acc = acc_ref[...]
    o_ref[0] = jnp.where(lane < HEAD_DIM, acc[:tq], acc[tq:]).astype(o_ref.dtype)


def _attn_prompt(q, k, v, sb_bias):
    b, t, _ = q.shape
    tq = min(TQ, t)
    n_pairs = ATT_DIM // HEAD_PAIR
    kv_spec = pl.BlockSpec((1, t, HEAD_PAIR), lambda bi, p, i: (bi, 0, p))
    q_spec = pl.BlockSpec((1, tq, HEAD_PAIR), lambda bi, p, i: (bi, i, p))
    return pl.pallas_call(
        _attn_prompt_body,
        grid=(b, n_pairs, t // tq),
        in_specs=[pl.BlockSpec(memory_space=pltpu.SMEM), q_spec, kv_spec, kv_spec,
                  pl.BlockSpec((2 * KB, KB), lambda bi, p, i: (0, 0))],
        out_specs=q_spec,
        out_shape=jax.ShapeDtypeStruct((b, t, ATT_DIM), BF16),
        scratch_shapes=[pltpu.VMEM((2 * tq, HEAD_PAIR), F32), pltpu.VMEM((2 * tq, LANES), F32)],
        compiler_params=_params(("arbitrary", "arbitrary", "arbitrary")),
        name="attn_prompt",
    )(sb_bias, q, k, v, _tri(KB))


def _attn_sample_body(n_steps, pt_ref, bias_ref, q_ref, kn_ref, vn_ref, *refs):
    del pt_ref
    pps = PAGES_PER_STEP
    k_refs = refs[:pps]
    v_refs = refs[pps:2 * pps]
    t_ref, o_ref, acc_ref, carry_ref, knew_ref, vnew_ref = refs[2 * pps:]
    j = pl.program_id(1)
    page = t_ref.shape[1]
    rows = N_HEADS * Q_PAD
    tri = t_ref[...]

    def block(load_k, load_v, masked):
        z = jnp.concatenate(
            [_dot_nt(q_ref[0, h], load_k(h).astype(BF16)) + bias_ref[h] for h in range(N_HEADS)], axis=0)
        sp = _softplus(z)
        if masked:
            r = lax.broadcasted_iota(jnp.int32, (rows, page), 0)
            col = lax.broadcasted_iota(jnp.int32, (rows, page), 1)
            mask = col < lax.rem(r, Q_PAD)
            sp = jnp.where(mask, sp, 0.0)
        hi, lo = _split_bf16(sp)
        c = _dot(jnp.concatenate([hi, lo], axis=1), tri) + carry_ref[...]
        a = jnp.exp(z - c)
        if masked:
            a = jnp.where(mask, a, 0.0)
        ab = a.astype(BF16)
        for h in range(N_HEADS):
            acc_ref[h] += _dot(ab[h * Q_PAD:(h + 1) * Q_PAD], load_v(h).astype(BF16))
        carry_ref[...] = jnp.broadcast_to(c[:, 0:1], carry_ref.shape)

    @pl.when(j == 0)
    def _():
        acc_ref[...] = jnp.zeros_like(acc_ref)
        carry_ref[...] = jnp.zeros_like(carry_ref)
        knew_ref[...] = jnp.zeros_like(knew_ref)
        vnew_ref[...] = jnp.zeros_like(vnew_ref)
        for h in range(N_HEADS):
            knew_ref[h, 0:8, :] = kn_ref[0, h]
            vnew_ref[h, 0:8, :] = vn_ref[0, h]
        block(lambda h: knew_ref[h], lambda h: vnew_ref[h], True)

    for s in range(pps):
        kr, vr = k_refs[s], v_refs[s]
        block(lambda h, kr=kr: kr[0, pl.ds(h, page, stride=N_HEADS), :],
              lambda h, vr=vr: vr[0, pl.ds(h, page, stride=N_HEADS), :], False)

    @pl.when(j == n_steps - 1)
    def _():
        o_ref[0] = acc_ref[...]


def _attn_sample(q, k_new, v_new, cache_k, cache_v, page_table, sb_bias):
    n_seq = q.shape[0]
    page = cache_k.shape[1] // N_HEADS
    n_pages = page_table.shape[1]
    pps = PAGES_PER_STEP
    n_steps = n_pages // pps

    def page_spec(s):
        return pl.BlockSpec((1, page * N_HEADS, HEAD_DIM),
                            lambda b, j, pt: (pt[b, n_pages - 1 - (j * pps + s)], 0, 0))

    seq4 = lambda r: pl.BlockSpec((1, N_HEADS, r, HEAD_DIM), lambda b, j, pt: (b, 0, 0, 0))
    grid_spec = pltpu.PrefetchScalarGridSpec(
        num_scalar_prefetch=1,
        grid=(n_seq, n_steps),
        in_specs=[pl.BlockSpec(memory_space=pltpu.SMEM), seq4(Q_PAD), seq4(8), seq4(8)]
                 + [page_spec(s) for s in range(pps)] * 2
                 + [pl.BlockSpec((2 * page, page), lambda b, j, pt: (0, 0))],
        out_specs=seq4(Q_PAD),
        scratch_shapes=[pltpu.VMEM((N_HEADS, Q_PAD, HEAD_DIM), F32),
                        pltpu.VMEM((N_HEADS * Q_PAD, LANES), F32),
                        pltpu.VMEM((N_HEADS, page, HEAD_DIM), F32),
                        pltpu.VMEM((N_HEADS, page, HEAD_DIM), F32)],
    )
    return pl.pallas_call(
        functools.partial(_attn_sample_body, n_steps),
        grid_spec=grid_spec,
        out_shape=jax.ShapeDtypeStruct((n_seq, N_HEADS, Q_PAD, HEAD_DIM), F32),
        compiler_params=_params(("arbitrary", "arbitrary")),
        name="attn_sample",
    )(page_table, sb_bias, q, k_new, v_new, *([cache_k] * pps), *([cache_v] * pps), _tri(page))


def _merge_body(x_ref, att_ref, ga_ref, cb_ref, wba_ref, wout_ref, g2_ref, wrh_ref, wrl_ref, rb_ref,
                h_ref, xn_ref, gate_ref):
    merged = ga_ref[...].astype(F32) * _dot(att_ref[...], wba_ref[...]) + cb_ref[...].astype(F32)
    h = x_ref[...] + _dot(merged.astype(BF16), wout_ref[...])
    h_ref[...] = h
    xn = _rms(h, g2_ref[...])
    xh, xl = _split_bf16(xn)
    xn_ref[...] = xh
    wrh = wrh_ref[...]
    logits = _dot(xh, wrh) + _dot(xl, wrh) + _dot(xh, wrl_ref[...])
    scores = _sigmoid(logits)
    work = scores + rb_ref[...]
    n_exp = work.shape[1]
    lane = lax.broadcasted_iota(jnp.int32, work.shape, 1)
    chosen = jnp.zeros(work.shape, jnp.bool_)
    for _ in range(TOP_K):
        m = jnp.max(work, axis=-1, keepdims=True)
        first = jnp.min(jnp.where(work == m, lane, n_exp), axis=-1, keepdims=True)
        sel = lane == first
        chosen = jnp.logical_or(chosen, sel)
        work = jnp.where(sel, -jnp.inf, work)
    picked = jnp.where(chosen, scores, 0.0)
    gate_ref[...] = picked / jnp.sum(picked, axis=-1, keepdims=True) * ROUTED_SCALE


def _merge(x, att, ga, cb, wba_bf, wout_bf, g2, wr_hi, wr_lo, rbias):
    n, d = x.shape
    tm = min(TM_PROJ, n)
    n_exp = wr_hi.shape[1]
    tok = lambda w: pl.BlockSpec((tm, w), lambda i: (i, 0))
    full = lambda shp: pl.BlockSpec(shp, lambda i: (0,) * len(shp))
    return pl.pallas_call(
        _merge_body,
        grid=(n // tm,),
        in_specs=[tok(d), tok(ATT_DIM), tok(d), tok(d), full((ATT_DIM, d)), full((d, d)), full((1, d)),
                  full((d, n_exp)), full((d, n_exp)), full((1, n_exp))],
        out_specs=(tok(d), tok(d), tok(n_exp)),
        out_shape=(jax.ShapeDtypeStruct((n, d), F32), jax.ShapeDtypeStruct((n, d), BF16),
                   jax.ShapeDtypeStruct((n, n_exp), F32)),
        compiler_params=_params(("arbitrary",)),
        name="merge_router",
    )(x, att, ga, cb, wba_bf, wout_bf, g2, wr_hi, wr_lo, rbias)


def _swiglu(x, wg, wu, wd):
    a = _dot(x, wg)
    hdn = (a * _sigmoid(a)) * _dot(x, wu)
    return _dot(hdn.astype(BF16), wd)


def _moe_body(xn_ref, h_ref, gate_ref, wg_ref, wu_ref, wd_ref, wgs_ref, wus_ref, wds_ref, gf_ref,
              y_ref, acc_ref):
    e = pl.program_id(1)
    x = xn_ref[...]

    @pl.when(e == 0)
    def _():
        acc_ref[...] = _swiglu(x, wgs_ref[...], wus_ref[...], wds_ref[...])

    gate = gate_ref[...]
    lane = lax.broadcasted_iota(jnp.int32, gate.shape, 1)
    g_e = jnp.sum(jnp.where(lane == e, gate, 0.0), axis=-1, keepdims=True)
    acc_ref[...] += g_e * _swiglu(x, wg_ref[0], wu_ref[0], wd_ref[0])

    @pl.when(e == pl.num_programs(1) - 1)
    def _():
        y_ref[...] = _rms(h_ref[...] + acc_ref[...], gf_ref[...])


def _moe(xn, h, gate, wg_bf, wu_bf, wd_bf, wgs_bf, wus_bf, wds_bf, gf):
    n, d = h.shape
    tm = min(TM_MOE, n)
    n_exp, _, de = wg_bf.shape
    ds = wgs_bf.shape[1]
    tok = lambda w: pl.BlockSpec((tm, w), lambda i, e: (i, 0))
    full = lambda shp: pl.BlockSpec(shp, lambda i, e: (0,) * len(shp))
    return pl.pallas_call(
        _moe_body,
        grid=(n // tm, n_exp),
        in_specs=[tok(d), tok(d), tok(n_exp),
                  pl.BlockSpec((1, d, de), lambda i, e: (e, 0, 0)),
                  pl.BlockSpec((1, d, de), lambda i, e: (e, 0, 0)),
                  pl.BlockSpec((1, de, d), lambda i, e: (e, 0, 0)),
                  full((d, ds)), full((d, ds)), full((ds, d)), full((1, d))],
        out_specs=tok(d),
        out_shape=jax.ShapeDtypeStruct((n, d), F32),
        scratch_shapes=[pltpu.VMEM((tm, d), F32)],
        compiler_params=_params(("arbitrary", "arbitrary")),
        name="moe",
    )(xn, h, gate, wg_bf, wu_bf, wd_bf, wgs_bf, wus_bf, wds_bf, gf)


def kernel(x_prompt, x_sample, cache_k, cache_v, state_conv, page_table, norm_mix_g, w_in, sb_bias, conv_w,
           w_branch_att, w_branch_conv, w_out, norm_ffn_g, w_router, router_bias, w_gate_e, w_up_e, w_down_e,
           w_gate_s, w_up_s, w_down_s, norm_final_g):
    depth = w_in.shape[0]
    assert depth == 1, "single trunk layer"
    b, t, d = x_prompt.shape
    db, ts, _ = x_sample.shape
    assert ts <= 8 and (CONV_DIM, ATT_DIM) == (state_conv.shape[-1], cache_k.shape[-2] * cache_k.shape[-1])
    l = 0
    row = lambda g: g.reshape(1, -1)
    w_in_bf = w_in[l].astype(BF16)
    wbc_bf = w_branch_conv[l].astype(BF16)
    wba_bf = w_branch_att[l].astype(BF16)
    wout_bf = w_out[l].astype(BF16)
    wr_hi = w_router[l].astype(BF16)
    wr_lo = (w_router[l] - wr_hi.astype(F32)).astype(BF16)
    moe_w = tuple(w[l].astype(BF16) for w in (w_gate_e, w_up_e, w_down_e, w_gate_s, w_up_s, w_down_s))
    bias = sb_bias[l]

    def channel_mixer(x2d, att, ga, cb):
        h, xn, gate = _merge(x2d, att, ga, cb, wba_bf, wout_bf, row(norm_ffn_g[l]), wr_hi, wr_lo,
                             row(router_bias[l]))
        return _moe(xn, h, gate, *moe_w, row(norm_final_g))

    q, k_bf, v_bf, k_f, v_f, ga, cb, conv_tail = _inproj_prompt(
        x_prompt, row(norm_mix_g[l]), w_in_bf, conv_w[l], wbc_bf)
    att = _attn_prompt(q, k_bf, v_bf, bias)
    n = b * t
    y_prompt = channel_mixer(x_prompt.reshape(n, d), att.reshape(n, ATT_DIM), ga.reshape(n, d),
                             cb.reshape(n, d)).reshape(b, t, d)

    ns = db * ts
    st = state_conv[l]
    n_hist = st.shape[1]
    hist1 = jnp.broadcast_to(st[:, n_hist - 1:n_hist], (db, ts, CONV_DIM)).reshape(ns, CONV_DIM)
    hist2 = jnp.concatenate([st, jnp.zeros((db, ts - n_hist, CONV_DIM), F32)], axis=1).reshape(ns, CONV_DIM)
    q_s, k_s, v_s, u_s, ga_s, cb_s = _inproj_sample(
        x_sample.reshape(ns, d), row(norm_mix_g[l]), w_in_bf, conv_w[l], wbc_bf, hist1, hist2, ts)

    def heads_first(a, pad_to, dtype):
        a = a.reshape(db, ts, N_HEADS, HEAD_DIM).transpose(0, 2, 1, 3)
        return jnp.pad(a, ((0, 0), (0, 0), (0, pad_to - ts), (0, 0))).astype(dtype)

    n_pool, page = cache_k.shape[1], cache_k.shape[2]
    att_s = _attn_sample(heads_first(q_s, Q_PAD, BF16), heads_first(k_s, 8, F32), heads_first(v_s, 8, F32),
                         cache_k[l].reshape(n_pool, page * N_HEADS, HEAD_DIM),
                         cache_v[l].reshape(n_pool, page * N_HEADS, HEAD_DIM), page_table, bias)
    att_s = att_s[:, :, :ts].transpose(0, 2, 1, 3).reshape(ns, ATT_DIM).astype(BF16)
    y_sample = channel_mixer(x_sample.reshape(ns, d), att_s, ga_s, cb_s).reshape(db, ts, d)

    heads = lambda a, lead: a.reshape(1, *lead, N_HEADS, HEAD_DIM)
    return (y_prompt, y_sample,
            heads(k_f, (b, t)), heads(v_f, (b, t)), conv_tail[:, 6:8].reshape(1, b, 2, CONV_DIM),
            heads(k_s, (db, ts)), heads(v_s, (db, ts)),
            u_s.reshape(db, ts, CONV_DIM)[:, ts - 2:].reshape(1, db, 2, CONV_DIM))
```

```python
import functools

import jax
import jax.numpy as jnp
import numpy as np
from jax import lax
from jax.experimental import pallas as pl
from jax.experimental.pallas import tpu as pltpu

F32 = jnp.float32
BF16 = jnp.bfloat16

N_HEADS = 8
HEAD_DIM = 64
ATT_DIM = N_HEADS * HEAD_DIM
CONV_DIM = 512
TOP_K = 8
ROUTED_SCALE = 2.5
EPS = 1e-6

VMEM_LIMIT_BYTES = 56 * 1024 * 1024
LANES = 128
HEAD_PAIR = LANES

TM_PROJ = 512
TM_MOE = 1024
TQ = 128
KB = 256
Q_PAD = 16
PAGES_PER_STEP = 8
PIPE = 4
PIPE_SLOTS = PIPE + 1
MASKED_LOGIT = -1e30


def _params(sem):
    return pltpu.CompilerParams(dimension_semantics=sem, vmem_limit_bytes=VMEM_LIMIT_BYTES)


def _rms(x, g):
    return (x * lax.rsqrt(jnp.mean(x * x, axis=-1, keepdims=True) + EPS)) * g


def _sigmoid(x):
    return 1.0 / (1.0 + jnp.exp(-x))


def _softplus(z):
    return jnp.maximum(z, 0.0) + jnp.log(1.0 + jnp.exp(-jnp.abs(z)))


def _dot(a, b):
    return jnp.dot(a, b, preferred_element_type=F32)


def _dot_nt(a, b):
    return lax.dot_general(a, b, (((1,), (1,)), ((), ())), preferred_element_type=F32)


def _split_bf16(x):
    hi = x.astype(BF16)
    lo = (x - hi.astype(F32)).astype(BF16)
    return hi, lo


def _tri(n):
    j = lax.broadcasted_iota(jnp.int32, (n, n), 0)
    s = lax.broadcasted_iota(jnp.int32, (n, n), 1)
    return (j > s).astype(BF16)


def _project(xb, w_ref, lo, hi):
    return _dot(xb, w_ref[:, lo:hi])


def _conv_branch(xb, w_ref, cw_ref, wbc_ref, u, um1, um2):
    a = ATT_DIM
    cw = cw_ref[...]
    conv = _project(xb, w_ref, 3 * a, 3 * a + CONV_DIM) * (cw[0:1] * um2 + cw[1:2] * um1 + cw[2:3] * u)
    d = w_ref.shape[0]
    g_conv = _project(xb, w_ref, 3 * a + 3 * CONV_DIM + d, 3 * a + 3 * CONV_DIM + 2 * d)
    return _sigmoid(g_conv) * _dot(conv.astype(BF16), wbc_ref[...])


def _inproj_prompt_body(x_ref, g_ref, w_ref, cw_ref, wbc_ref,
                        q_ref, kb_ref, vb_ref, kf_ref, vf_ref, ga_ref, cb_ref, cs_ref, hist_ref):
    a, c, d = ATT_DIM, CONV_DIM, x_ref.shape[2]
    tm = x_ref.shape[1]
    xb = _rms(x_ref[0], g_ref[...]).astype(BF16)
    q_ref[0] = (_project(xb, w_ref, 0, a) * (HEAD_DIM ** -0.5)).astype(BF16)
    k = _project(xb, w_ref, a, 2 * a)
    kf_ref[0] = k
    kb_ref[0] = k.astype(BF16)
    v = _project(xb, w_ref, 2 * a, 3 * a)
    vf_ref[0] = v
    vb_ref[0] = v.astype(BF16)
    u = _project(xb, w_ref, 3 * a + c, 3 * a + 2 * c) * _project(xb, w_ref, 3 * a + 2 * c, 3 * a + 3 * c)

    @pl.when(pl.program_id(1) == 0)
    def _():
        hist_ref[...] = jnp.zeros_like(hist_ref)

    row = lax.broadcasted_iota(jnp.int32, u.shape, 0)
    h0 = hist_ref[6:7, :]
    h1 = hist_ref[7:8, :]
    um1 = jnp.where(row < 1, h1, pltpu.roll(u, 1, 0))
    um2 = jnp.where(row < 1, h0, jnp.where(row < 2, h1, pltpu.roll(u, 2, 0)))
    cb_ref[0] = _conv_branch(xb, w_ref, cw_ref, wbc_ref, u, um1, um2).astype(BF16)
    tail = u[tm - 8:tm, :]
    hist_ref[...] = tail
    cs_ref[0] = tail
    ga_ref[0] = _sigmoid(_project(xb, w_ref, 3 * a + 3 * c, 3 * a + 3 * c + d)).astype(BF16)


def _inproj_prompt(x, g, w_bf, conv_w, wbc_bf):
    b, t, d = x.shape
    tm = min(TM_PROJ, t)
    nt = t // tm
    n_in = w_bf.shape[1]
    tok = lambda n: pl.BlockSpec((1, tm, n), lambda i, j: (i, j, 0))
    full = lambda shp: pl.BlockSpec(shp, lambda i, j: (0,) * len(shp))
    out_shape = (
        jax.ShapeDtypeStruct((b, t, ATT_DIM), BF16),
        jax.ShapeDtypeStruct((b, t, ATT_DIM), BF16),
        jax.ShapeDtypeStruct((b, t, ATT_DIM), BF16),
        jax.ShapeDtypeStruct((b, t, ATT_DIM), F32),
        jax.ShapeDtypeStruct((b, t, ATT_DIM), F32),
        jax.ShapeDtypeStruct((b, t, d), BF16),
        jax.ShapeDtypeStruct((b, t, d), BF16),
        jax.ShapeDtypeStruct((b, 8, CONV_DIM), F32),
    )
    return pl.pallas_call(
        _inproj_prompt_body,
        grid=(b, nt),
        in_specs=[tok(d), full((1, d)), full((d, n_in)), full((3, CONV_DIM)), full((CONV_DIM, d))],
        out_specs=(tok(ATT_DIM), tok(ATT_DIM), tok(ATT_DIM), tok(ATT_DIM), tok(ATT_DIM), tok(d), tok(d),
                   pl.BlockSpec((1, 8, CONV_DIM), lambda i, j: (i, 0, 0))),
        out_shape=out_shape,
        scratch_shapes=[pltpu.VMEM((8, CONV_DIM), F32)],
        compiler_params=_params(("arbitrary", "arbitrary")),
        name="inproj_prompt",
    )(x, g, w_bf, conv_w, wbc_bf)


def _inproj_sample_body(ts, x_ref, g_ref, w_ref, cw_ref, wbc_ref, h1_ref, h2_ref,
                        q_ref, k_ref, v_ref, u_ref, ga_ref, cb_ref):
    a, c, d = ATT_DIM, CONV_DIM, x_ref.shape[1]
    xb = _rms(x_ref[...], g_ref[...]).astype(BF16)
    q_ref[...] = _project(xb, w_ref, 0, a) * (HEAD_DIM ** -0.5)
    k_ref[...] = _project(xb, w_ref, a, 2 * a)
    v_ref[...] = _project(xb, w_ref, 2 * a, 3 * a)
    u = _project(xb, w_ref, 3 * a + c, 3 * a + 2 * c) * _project(xb, w_ref, 3 * a + 2 * c, 3 * a + 3 * c)
    u_ref[...] = u
    tok = lax.rem(lax.broadcasted_iota(jnp.int32, u.shape, 0), ts)
    um1 = jnp.where(tok >= 1, pltpu.roll(u, 1, 0), h1_ref[...])
    um2 = jnp.where(tok >= 2, pltpu.roll(u, 2, 0), h2_ref[...])
    cb_ref[...] = _conv_branch(xb, w_ref, cw_ref, wbc_ref, u, um1, um2).astype(BF16)
    ga_ref[...] = _sigmoid(_project(xb, w_ref, 3 * a + 3 * c, 3 * a + 3 * c + d)).astype(BF16)


def _inproj_sample(x, g, w_bf, conv_w, wbc_bf, hist1, hist2, ts):
    n, d = x.shape
    out_shape = (
        jax.ShapeDtypeStruct((n, ATT_DIM), F32),
        jax.ShapeDtypeStruct((n, ATT_DIM), F32),
        jax.ShapeDtypeStruct((n, ATT_DIM), F32),
        jax.ShapeDtypeStruct((n, CONV_DIM), F32),
        jax.ShapeDtypeStruct((n, d), BF16),
        jax.ShapeDtypeStruct((n, d), BF16),
    )
    return pl.pallas_call(
        functools.partial(_inproj_sample_body, ts),
        out_shape=out_shape,
        compiler_params=pltpu.CompilerParams(vmem_limit_bytes=VMEM_LIMIT_BYTES),
        name="inproj_sample",
    )(x, g, w_bf, conv_w, wbc_bf, hist1, hist2)


def _attn_schedule(t, tq, kb):
    items = []
    for i in range(t // tq):
        jd = (i * tq) // kb
        for n in range(jd + 1):
            diag = 1 + ((i * tq) % kb) // tq if n == 0 else 0
            items.append((i * tq, (jd - n) * kb, diag, int(n == 0), int(n == jd)))
    idle = (0, 0, 0, 1, 0)
    n_steps = -(-(len(items) + PIPE) // PIPE_SLOTS) * PIPE_SLOTS
    assert items[0][3] == 1 and items[0][4] == 1
    items = [idle] * PIPE + items + [items[0]] * (n_steps - len(items))
    return np.asarray(items, np.int32).T, n_steps


def _attn_prompt_body(n_steps, tab_ref, bias_ref, q_ref, k_ref, v_ref, t_ref, o_ref,
                      z_ref, sp_ref, d_ref, a_ref, bm_ref, acc_ref, carry_ref):
    p = pl.program_id(1)
    rows, kb = bm_ref.shape[1], bm_ref.shape[2]
    tq = rows // 2
    n_diag = bm_ref.shape[0] - 1

    r = lax.broadcasted_iota(jnp.int32, (rows, kb), 0)
    col = lax.broadcasted_iota(jnp.int32, (rows, kb), 1)
    bias = jnp.where(r < tq, bias_ref[2 * p], bias_ref[2 * p + 1])
    bm_ref[0] = bias
    for dgn in range(n_diag):
        bm_ref[1 + dgn] = jnp.where(col < dgn * tq + lax.rem(r, tq), bias, MASKED_LOGIT)
    z_ref[...] = jnp.full(z_ref.shape, MASKED_LOGIT, F32)
    sp_ref[...] = jnp.zeros_like(sp_ref)
    d_ref[...] = jnp.zeros_like(d_ref)
    a_ref[...] = jnp.zeros_like(a_ref)
    acc_ref[...] = jnp.zeros_like(acc_ref)
    carry_ref[...] = jnp.zeros_like(carry_ref)
    lane = lax.broadcasted_iota(jnp.int32, (tq, HEAD_PAIR), 1)

    def logits(e, slot):
        q = q_ref[0, pl.ds(pl.multiple_of(tab_ref[0, e], tq), tq), :]
        zero = jnp.zeros_like(q)
        q2 = jnp.concatenate([jnp.where(lane < HEAD_DIM, q, zero), jnp.where(lane >= HEAD_DIM, q, zero)], axis=0)
        kblk = k_ref[0, pl.ds(pl.multiple_of(tab_ref[1, e], kb), kb), :]
        z_ref[slot] = _dot_nt(q2, kblk) + bm_ref[tab_ref[2, e]]

    def softplus(slot):
        z = z_ref[slot]
        sp = _softplus(z)
        z_ref[slot] = z - sp
        sp_ref[slot] = sp.astype(BF16)

    def block_cumsum(slot):
        d_ref[slot] = _dot(sp_ref[slot], t_ref[...])

    def attention_weights(e, slot):
        carry = jnp.where(tab_ref[3, e] == 1, 0.0, carry_ref[...])
        d = d_ref[slot]
        c = d + jnp.concatenate([carry] * (kb // LANES), axis=1)
        a_ref[slot] = jnp.exp(z_ref[slot] - c).astype(BF16)
        total = d[:, 0:1] + sp_ref[slot, :, 0:1].astype(F32)
        carry_ref[...] = carry + jnp.broadcast_to(total, carry_ref.shape)

    def weights_times_values(e, slot):
        vblk = v_ref[0, pl.ds(pl.multiple_of(tab_ref[1, e], kb), kb), :]
        acc = jnp.where(tab_ref[3, e] == 1, 0.0, acc_ref[...]) + _dot(a_ref[slot], vblk)
        acc_ref[...] = acc
        o_ref[0, pl.ds(pl.multiple_of(tab_ref[0, e], tq), tq), :] = jnp.where(
            lane < HEAD_DIM, acc[:tq], acc[tq:]).astype(o_ref.dtype)

    def body(it, _):
        for u in range(PIPE_SLOTS):
            e = it * PIPE_SLOTS + u
            slot = lambda s, u=u: (u + s) % PIPE_SLOTS
            weights_times_values(e, slot(0))
            attention_weights(e + 1, slot(1))
            block_cumsum(slot(2))
            softplus(slot(3))
            logits(e + 4, slot(4))
        return 0

    lax.fori_loop(0, n_steps // PIPE_SLOTS, body, 0)


def _attn_prompt(q, k, v, sb_bias):
    b, t, _ = q.shape
    tq = min(TQ, t)
    kb = min(KB, t)
    n_pairs = ATT_DIM // HEAD_PAIR
    tab, n_steps = _attn_schedule(t, tq, kb)
    seq_spec = pl.BlockSpec((1, t, HEAD_PAIR), lambda bi, p, tab: (bi, 0, p))
    grid_spec = pltpu.PrefetchScalarGridSpec(
        num_scalar_prefetch=1,
        grid=(b, n_pairs),
        in_specs=[pl.BlockSpec(memory_space=pltpu.SMEM), seq_spec, seq_spec, seq_spec,
                  pl.BlockSpec((kb, kb), lambda bi, p, tab: (0, 0))],
        out_specs=seq_spec,
        scratch_shapes=[pltpu.VMEM((PIPE_SLOTS, 2 * tq, kb), F32),
                        pltpu.VMEM((PIPE_SLOTS, 2 * tq, kb), BF16),
                        pltpu.VMEM((PIPE_SLOTS, 2 * tq, kb), F32),
                        pltpu.VMEM((PIPE_SLOTS, 2 * tq, kb), BF16),
                        pltpu.VMEM((1 + kb // tq, 2 * tq, kb), F32),
                        pltpu.VMEM((2 * tq, HEAD_PAIR), F32),
                        pltpu.VMEM((2 * tq, LANES), F32)],
    )
    return pl.pallas_call(
        functools.partial(_attn_prompt_body, n_steps),
        grid_spec=grid_spec,
        out_shape=jax.ShapeDtypeStruct((b, t, ATT_DIM), BF16),
        compiler_params=_params(("arbitrary", "arbitrary")),
        name="attn_prompt",
    )(jnp.asarray(tab), sb_bias, q, k, v, _tri(kb))


def _attn_sample_body(n_steps, pps, pt_ref, bias_ref, q_ref, kn_ref, vn_ref, *refs):
    del pt_ref
    k_refs = refs[:pps]
    v_refs = refs[pps:2 * pps]
    t_ref, o_ref, acc_ref, carry_ref, knew_ref, vnew_ref = refs[2 * pps:]
    j = pl.program_id(1)
    page = t_ref.shape[1]
    rows = N_HEADS * Q_PAD

    def blocks(loaders, masked):
        z = jnp.concatenate(
            [_dot_nt(q_ref[0, h], load_k(h).astype(BF16)) + bias_ref[h]
             for load_k, _ in loaders for h in range(N_HEADS)], axis=0)
        if masked:
            r = lax.broadcasted_iota(jnp.int32, z.shape, 0)
            col = lax.broadcasted_iota(jnp.int32, z.shape, 1)
            z = jnp.where(col < lax.rem(r, Q_PAD), z, MASKED_LOGIT)
        sp = _softplus(z)
        own = z - sp
        sp = sp.astype(BF16)
        d = _dot(sp, t_ref[...])
        carry = carry_ref[...]
        for s, (_, load_v) in enumerate(loaders):
            blk = slice(s * rows, (s + 1) * rows)
            ab = jnp.exp(own[blk] - (d[blk] + carry)).astype(BF16)
            for h in range(N_HEADS):
                acc_ref[h] += _dot(ab[h * Q_PAD:(h + 1) * Q_PAD], load_v(h).astype(BF16))
            total = d[blk, 0:1] + sp[blk, 0:1].astype(F32)
            carry = carry + jnp.broadcast_to(total, carry.shape)
        carry_ref[...] = carry

    @pl.when(j == 0)
    def _():
        acc_ref[...] = jnp.zeros_like(acc_ref)
        carry_ref[...] = jnp.zeros_like(carry_ref)
        knew_ref[...] = jnp.zeros_like(knew_ref)
        vnew_ref[...] = jnp.zeros_like(vnew_ref)
        for h in range(N_HEADS):
            knew_ref[h, 0:8, :] = kn_ref[0, h]
            vnew_ref[h, 0:8, :] = vn_ref[0, h]
        blocks([(lambda h: knew_ref[h], lambda h: vnew_ref[h])], True)

    def head_rows(ref):
        flat = ref.reshape(page * N_HEADS, HEAD_DIM)
        return lambda h: flat[pl.ds(h, page, stride=N_HEADS), :]

    blocks([(head_rows(kr), head_rows(vr)) for kr, vr in zip(k_refs, v_refs)], False)

    @pl.when(j == n_steps - 1)
    def _():
        o_ref[0] = acc_ref[...]


def _attn_sample(q, k_new, v_new, cache_k, cache_v, page_table, sb_bias):
    n_seq = q.shape[0]
    page = cache_k.shape[2]
    n_pages = page_table.shape[1]
    pps = min(PAGES_PER_STEP, n_pages)
    n_steps = n_pages // pps

    def page_spec(s):
        return pl.BlockSpec((None, None, page, N_HEADS, HEAD_DIM),
                            lambda b, j, pt: (0, pt[b, n_pages - 1 - (j * pps + s)], 0, 0, 0))

    seq4 = lambda r: pl.BlockSpec((1, N_HEADS, r, HEAD_DIM), lambda b, j, pt: (b, 0, 0, 0))
    grid_spec = pltpu.PrefetchScalarGridSpec(
        num_scalar_prefetch=1,
        grid=(n_seq, n_steps),
        in_specs=[pl.BlockSpec(memory_space=pltpu.SMEM), seq4(Q_PAD), seq4(8), seq4(8)]
                 + [page_spec(s) for s in range(pps)] * 2
                 + [pl.BlockSpec((page, page), lambda b, j, pt: (0, 0))],
        out_specs=seq4(Q_PAD),
        scratch_shapes=[pltpu.VMEM((N_HEADS, Q_PAD, HEAD_DIM), F32),
                        pltpu.VMEM((N_HEADS * Q_PAD, LANES), F32),
                        pltpu.VMEM((N_HEADS, page, HEAD_DIM), F32),
                        pltpu.VMEM((N_HEADS, page, HEAD_DIM), F32)],
    )
    return pl.pallas_call(
        functools.partial(_attn_sample_body, n_steps, pps),
        grid_spec=grid_spec,
        out_shape=jax.ShapeDtypeStruct((n_seq, N_HEADS, Q_PAD, HEAD_DIM), F32),
        compiler_params=_params(("arbitrary", "arbitrary")),
        name="attn_sample",
    )(page_table, sb_bias, q, k_new, v_new, *([cache_k] * pps), *([cache_v] * pps), _tri(page))


def _merge_body(x_ref, att_ref, ga_ref, cb_ref, wba_ref, wout_ref, g2_ref, wrh_ref, wrl_ref, rb_ref,
                h_ref, xn_ref, gate_ref):
    merged = ga_ref[...].astype(F32) * _dot(att_ref[...], wba_ref[...]) + cb_ref[...].astype(F32)
    h = x_ref[...] + _dot(merged.astype(BF16), wout_ref[...])
    h_ref[...] = h
    xn = _rms(h, g2_ref[...])
    xh, xl = _split_bf16(xn)
    xn_ref[...] = xh
    wrh = wrh_ref[...]
    logits = _dot(xh, wrh) + _dot(xl, wrh) + _dot(xh, wrl_ref[...])
    scores = _sigmoid(logits)
    work = scores + rb_ref[...]
    n_exp = work.shape[1]
    lane = lax.broadcasted_iota(jnp.int32, work.shape, 1)
    chosen = jnp.zeros(work.shape, jnp.bool_)
    for _ in range(TOP_K):
        m = jnp.max(work, axis=-1, keepdims=True)
        first = jnp.min(jnp.where(work == m, lane, n_exp), axis=-1, keepdims=True)
        sel = lane == first
        chosen = jnp.logical_or(chosen, sel)
        work = jnp.where(sel, -jnp.inf, work)
    picked = jnp.where(chosen, scores, 0.0)
    gate_ref[...] = picked / jnp.sum(picked, axis=-1, keepdims=True) * ROUTED_SCALE


def _merge(x, att, ga, cb, wba_bf, wout_bf, g2, wr_hi, wr_lo, rbias):
    n, d = x.shape
    tm = min(TM_PROJ, n)
    n_exp = wr_hi.shape[1]
    tok = lambda w: pl.BlockSpec((tm, w), lambda i: (i, 0))
    full = lambda shp: pl.BlockSpec(shp, lambda i: (0,) * len(shp))
    return pl.pallas_call(
        _merge_body,
        grid=(n // tm,),
        in_specs=[tok(d), tok(ATT_DIM), tok(d), tok(d), full((ATT_DIM, d)), full((d, d)), full((1, d)),
                  full((d, n_exp)), full((d, n_exp)), full((1, n_exp))],
        out_specs=(tok(d), tok(d), tok(n_exp)),
        out_shape=(jax.ShapeDtypeStruct((n, d), F32), jax.ShapeDtypeStruct((n, d), BF16),
                   jax.ShapeDtypeStruct((n, n_exp), F32)),
        compiler_params=_params(("arbitrary",)),
        name="merge_router",
    )(x, att, ga, cb, wba_bf, wout_bf, g2, wr_hi, wr_lo, rbias)


def _swiglu(x, wg, wu, wd):
    a = _dot(x, wg)
    hdn = (a * _sigmoid(a)) * _dot(x, wu)
    return _dot(hdn.astype(BF16), wd)


def _moe_body(xn_ref, h_ref, gate_ref, wg_ref, wu_ref, wd_ref, wgs_ref, wus_ref, wds_ref, gf_ref,
              y_ref, acc_ref):
    e = pl.program_id(1)
    x = xn_ref[...]

    @pl.when(e == 0)
    def _():
        acc_ref[...] = _swiglu(x, wgs_ref[...], wus_ref[...], wds_ref[...])

    gate = gate_ref[...]
    lane = lax.broadcasted_iota(jnp.int32, gate.shape, 1)
    g_e = jnp.sum(jnp.where(lane == e, gate, 0.0), axis=-1, keepdims=True)
    acc_ref[...] += g_e * _swiglu(x, wg_ref[0], wu_ref[0], wd_ref[0])

    @pl.when(e == pl.num_programs(1) - 1)
    def _():
        y_ref[...] = _rms(h_ref[...] + acc_ref[...], gf_ref[...])


def _moe(xn, h, gate, wg_bf, wu_bf, wd_bf, wgs_bf, wus_bf, wds_bf, gf):
    n, d = h.shape
    tm = min(TM_MOE, n)
    n_exp, _, de = wg_bf.shape
    ds = wgs_bf.shape[1]
    tok = lambda w: pl.BlockSpec((tm, w), lambda i, e: (i, 0))
    full = lambda shp: pl.BlockSpec(shp, lambda i, e: (0,) * len(shp))
    return pl.pallas_call(
        _moe_body,
        grid=(n // tm, n_exp),
        in_specs=[tok(d), tok(d), tok(n_exp),
                  pl.BlockSpec((1, d, de), lambda i, e: (e, 0, 0)),
                  pl.BlockSpec((1, d, de), lambda i, e: (e, 0, 0)),
                  pl.BlockSpec((1, de, d), lambda i, e: (e, 0, 0)),
                  full((d, ds)), full((d, ds)), full((ds, d)), full((1, d))],
        out_specs=tok(d),
        out_shape=jax.ShapeDtypeStruct((n, d), F32),
        scratch_shapes=[pltpu.VMEM((tm, d), F32)],
        compiler_params=_params(("arbitrary", "arbitrary")),
        name="moe",
    )(xn, h, gate, wg_bf, wu_bf, wd_bf, wgs_bf, wus_bf, wds_bf, gf)


def kernel(x_prompt, x_sample, cache_k, cache_v, state_conv, page_table, norm_mix_g, w_in, sb_bias, conv_w,
           w_branch_att, w_branch_conv, w_out, norm_ffn_g, w_router, router_bias, w_gate_e, w_up_e, w_down_e,
           w_gate_s, w_up_s, w_down_s, norm_final_g):
    depth = w_in.shape[0]
    assert depth == 1, "single trunk layer"
    b, t, d = x_prompt.shape
    db, ts, _ = x_sample.shape
    assert ts <= 8 and (CONV_DIM, ATT_DIM) == (state_conv.shape[-1], cache_k.shape[-2] * cache_k.shape[-1])
    l = 0
    row = lambda g: g.reshape(1, -1)
    w_in_bf = w_in[l].astype(BF16)
    wbc_bf = w_branch_conv[l].astype(BF16)
    wba_bf = w_branch_att[l].astype(BF16)
    wout_bf = w_out[l].astype(BF16)
    wr_hi = w_router[l].astype(BF16)
    wr_lo = (w_router[l] - wr_hi.astype(F32)).astype(BF16)
    moe_w = tuple(w[l].astype(BF16) for w in (w_gate_e, w_up_e, w_down_e, w_gate_s, w_up_s, w_down_s))
    bias = sb_bias[l]

    def channel_mixer(x2d, att, ga, cb):
        h, xn, gate = _merge(x2d, att, ga, cb, wba_bf, wout_bf, row(norm_ffn_g[l]), wr_hi, wr_lo,
                             row(router_bias[l]))
        return _moe(xn, h, gate, *moe_w, row(norm_final_g))

    q, k_bf, v_bf, k_f, v_f, ga, cb, conv_tail = _inproj_prompt(
        x_prompt, row(norm_mix_g[l]), w_in_bf, conv_w[l], wbc_bf)
    att = _attn_prompt(q, k_bf, v_bf, bias)
    n = b * t
    y_prompt = channel_mixer(x_prompt.reshape(n, d), att.reshape(n, ATT_DIM), ga.reshape(n, d),
                             cb.reshape(n, d)).reshape(b, t, d)

    ns = db * ts
    st = state_conv[l]
    n_hist = st.shape[1]
    hist1 = jnp.broadcast_to(st[:, n_hist - 1:n_hist], (db, ts, CONV_DIM)).reshape(ns, CONV_DIM)
    hist2 = jnp.concatenate([st, jnp.zeros((db, ts - n_hist, CONV_DIM), F32)], axis=1).reshape(ns, CONV_DIM)
    q_s, k_s, v_s, u_s, ga_s, cb_s = _inproj_sample(
        x_sample.reshape(ns, d), row(norm_mix_g[l]), w_in_bf, conv_w[l], wbc_bf, hist1, hist2, ts)

    def heads_first(a, pad_to, dtype):
        a = a.reshape(db, ts, N_HEADS, HEAD_DIM).transpose(0, 2, 1, 3)
        return jnp.pad(a, ((0, 0), (0, 0), (0, pad_to - ts), (0, 0))).astype(dtype)

    att_s = _attn_sample(heads_first(q_s, Q_PAD, BF16), heads_first(k_s, 8, F32), heads_first(v_s, 8, F32),
                         cache_k, cache_v, page_table, bias)
    att_s = att_s[:, :, :ts].transpose(0, 2, 1, 3).reshape(ns, ATT_DIM).astype(BF16)
    y_sample = channel_mixer(x_sample.reshape(ns, d), att_s, ga_s, cb_s).reshape(db, ts, d)

    heads = lambda a, lead: a.reshape(1, *lead, N_HEADS, HEAD_DIM)
    return (y_prompt, y_sample,
            heads(k_f, (b, t)), heads(v_f, (b, t)), conv_tail[:, 6:8].reshape(1, b, 2, CONV_DIM),
            heads(k_s, (db, ts)), heads(v_s, (db, ts)),
            u_s.reshape(db, ts, CONV_DIM)[:, ts - 2:].reshape(1, db, 2, CONV_DIM))
```

```python
import functools

import jax
import jax.numpy as jnp
import numpy as np
from jax import lax
from jax.experimental import pallas as pl
from jax.experimental.pallas import tpu as pltpu

F32 = jnp.float32
BF16 = jnp.bfloat16

N_HEADS = 8
HEAD_DIM = 64
ATT_DIM = N_HEADS * HEAD_DIM
CONV_DIM = 512
TOP_K = 8
ROUTED_SCALE = 2.5
EPS = 1e-6

VMEM_LIMIT_BYTES = 56 * 1024 * 1024
LANES = 128
HEAD_PAIR = LANES

TM_PROJ = 512
TM_MOE = 1024
TQ = 128
KB = 256
Q_PAD = 16
PAGES_PER_STEP = 8
PIPE = 4
PIPE_SLOTS = PIPE + 1
MASKED_LOGIT = -1e30


def _params(sem):
    return pltpu.CompilerParams(dimension_semantics=sem, vmem_limit_bytes=VMEM_LIMIT_BYTES)


def _rms(x, g):
    return (x * lax.rsqrt(jnp.mean(x * x, axis=-1, keepdims=True) + EPS)) * g


def _sigmoid(x):
    return 1.0 / (1.0 + jnp.exp(-x))


def _softplus(z):
    return jnp.maximum(z, 0.0) + jnp.log(1.0 + jnp.exp(-jnp.abs(z)))


def _dot(a, b):
    return jnp.dot(a, b, preferred_element_type=F32)


def _dot_nt(a, b):
    return lax.dot_general(a, b, (((1,), (1,)), ((), ())), preferred_element_type=F32)


def _split_bf16(x):
    hi = x.astype(BF16)
    lo = (x - hi.astype(F32)).astype(BF16)
    return hi, lo


def _tri(n):
    j = lax.broadcasted_iota(jnp.int32, (n, n), 0)
    s = lax.broadcasted_iota(jnp.int32, (n, n), 1)
    return (j > s).astype(BF16)


def _project(xb, w_ref, lo, hi):
    return _dot(xb, w_ref[:, lo:hi])


def _conv_branch(xb, w_ref, cw_ref, wbc_ref, u, um1, um2):
    a = ATT_DIM
    cw = cw_ref[...]
    conv = _project(xb, w_ref, 3 * a, 3 * a + CONV_DIM) * (cw[0:1] * um2 + cw[1:2] * um1 + cw[2:3] * u)
    d = w_ref.shape[0]
    g_conv = _project(xb, w_ref, 3 * a + 3 * CONV_DIM + d, 3 * a + 3 * CONV_DIM + 2 * d)
    return _sigmoid(g_conv) * _dot(conv.astype(BF16), wbc_ref[...])


def _inproj_prompt_body(x_ref, g_ref, w_ref, cw_ref, wbc_ref,
                        q_ref, kb_ref, vb_ref, kf_ref, vf_ref, ga_ref, cb_ref, cs_ref, hist_ref):
    a, c, d = ATT_DIM, CONV_DIM, x_ref.shape[2]
    tm = x_ref.shape[1]
    xb = _rms(x_ref[0], g_ref[...]).astype(BF16)
    q_ref[0] = (_project(xb, w_ref, 0, a) * (HEAD_DIM ** -0.5)).astype(BF16)
    k = _project(xb, w_ref, a, 2 * a)
    kf_ref[0] = k
    kb_ref[0] = k.astype(BF16)
    v = _project(xb, w_ref, 2 * a, 3 * a)
    vf_ref[0] = v
    vb_ref[0] = v.astype(BF16)
    u = _project(xb, w_ref, 3 * a + c, 3 * a + 2 * c) * _project(xb, w_ref, 3 * a + 2 * c, 3 * a + 3 * c)

    @pl.when(pl.program_id(1) == 0)
    def _():
        hist_ref[...] = jnp.zeros_like(hist_ref)

    row = lax.broadcasted_iota(jnp.int32, u.shape, 0)
    h0 = hist_ref[6:7, :]
    h1 = hist_ref[7:8, :]
    um1 = jnp.where(row < 1, h1, pltpu.roll(u, 1, 0))
    um2 = jnp.where(row < 1, h0, jnp.where(row < 2, h1, pltpu.roll(u, 2, 0)))
    cb_ref[0] = _conv_branch(xb, w_ref, cw_ref, wbc_ref, u, um1, um2).astype(BF16)
    tail = u[tm - 8:tm, :]
    hist_ref[...] = tail
    cs_ref[0] = tail
    ga_ref[0] = _sigmoid(_project(xb, w_ref, 3 * a + 3 * c, 3 * a + 3 * c + d)).astype(BF16)


def _inproj_prompt(x, g, w_bf, conv_w, wbc_bf):
    b, t, d = x.shape
    tm = min(TM_PROJ, t)
    nt = t // tm
    n_in = w_bf.shape[1]
    tok = lambda n: pl.BlockSpec((1, tm, n), lambda i, j: (i, j, 0))
    full = lambda shp: pl.BlockSpec(shp, lambda i, j: (0,) * len(shp))
    out_shape = (
        jax.ShapeDtypeStruct((b, t, ATT_DIM), BF16),
        jax.ShapeDtypeStruct((b, t, ATT_DIM), BF16),
        jax.ShapeDtypeStruct((b, t, ATT_DIM), BF16),
        jax.ShapeDtypeStruct((b, t, ATT_DIM), F32),
        jax.ShapeDtypeStruct((b, t, ATT_DIM), F32),
        jax.ShapeDtypeStruct((b, t, d), BF16),
        jax.ShapeDtypeStruct((b, t, d), BF16),
        jax.ShapeDtypeStruct((b, 8, CONV_DIM), F32),
    )
    return pl.pallas_call(
        _inproj_prompt_body,
        grid=(b, nt),
        in_specs=[tok(d), full((1, d)), full((d, n_in)), full((3, CONV_DIM)), full((CONV_DIM, d))],
        out_specs=(tok(ATT_DIM), tok(ATT_DIM), tok(ATT_DIM), tok(ATT_DIM), tok(ATT_DIM), tok(d), tok(d),
                   pl.BlockSpec((1, 8, CONV_DIM), lambda i, j: (i, 0, 0))),
        out_shape=out_shape,
        scratch_shapes=[pltpu.VMEM((8, CONV_DIM), F32)],
        compiler_params=_params(("arbitrary", "arbitrary")),
        name="inproj_prompt",
    )(x, g, w_bf, conv_w, wbc_bf)


def _inproj_sample_body(ts, x_ref, g_ref, w_ref, cw_ref, wbc_ref, h1_ref, h2_ref,
                        q_ref, k_ref, v_ref, u_ref, ga_ref, cb_ref):
    a, c, d = ATT_DIM, CONV_DIM, x_ref.shape[1]
    xb = _rms(x_ref[...], g_ref[...]).astype(BF16)
    q_ref[...] = _project(xb, w_ref, 0, a) * (HEAD_DIM ** -0.5)
    k_ref[...] = _project(xb, w_ref, a, 2 * a)
    v_ref[...] = _project(xb, w_ref, 2 * a, 3 * a)
    u = _project(xb, w_ref, 3 * a + c, 3 * a + 2 * c) * _project(xb, w_ref, 3 * a + 2 * c, 3 * a + 3 * c)
    u_ref[...] = u
    tok = lax.rem(lax.broadcasted_iota(jnp.int32, u.shape, 0), ts)
    um1 = jnp.where(tok >= 1, pltpu.roll(u, 1, 0), h1_ref[...])
    um2 = jnp.where(tok >= 2, pltpu.roll(u, 2, 0), h2_ref[...])
    cb_ref[...] = _conv_branch(xb, w_ref, cw_ref, wbc_ref, u, um1, um2).astype(BF16)
    ga_ref[...] = _sigmoid(_project(xb, w_ref, 3 * a + 3 * c, 3 * a + 3 * c + d)).astype(BF16)


def _inproj_sample(x, g, w_bf, conv_w, wbc_bf, hist1, hist2, ts):
    n, d = x.shape
    out_shape = (
        jax.ShapeDtypeStruct((n, ATT_DIM), F32),
        jax.ShapeDtypeStruct((n, ATT_DIM), F32),
        jax.ShapeDtypeStruct((n, ATT_DIM), F32),
        jax.ShapeDtypeStruct((n, CONV_DIM), F32),
        jax.ShapeDtypeStruct((n, d), BF16),
        jax.ShapeDtypeStruct((n, d), BF16),
    )
    return pl.pallas_call(
        functools.partial(_inproj_sample_body, ts),
        out_shape=out_shape,
        compiler_params=pltpu.CompilerParams(vmem_limit_bytes=VMEM_LIMIT_BYTES),
        name="inproj_sample",
    )(x, g, w_bf, conv_w, wbc_bf, hist1, hist2)


def _attn_schedule(t, tq, kb):
    items = []
    for i in range(t // tq):
        jd = (i * tq) // kb
        for n in range(jd + 1):
            diag = 1 + ((i * tq) % kb) // tq if n == 0 else 0
            items.append((i * tq, (jd - n) * kb, diag, int(n == 0), int(n == jd)))
    idle = (0, 0, 0, 1, 0)
    n_steps = -(-(len(items) + PIPE) // PIPE_SLOTS) * PIPE_SLOTS
    assert items[0][3] == 1 and items[0][4] == 1
    items = [idle] * PIPE + items + [items[0]] * (n_steps - len(items))
    return np.asarray(items, np.int32).T, n_steps


def _attn_prompt_body(n_steps, tab_ref, bias_ref, q_ref, k_ref, v_ref, t_ref, o_ref,
                      z_ref, sp_ref, d_ref, a_ref, bm_ref, acc_ref, carry_ref):
    p = pl.program_id(1)
    rows, kb = bm_ref.shape[1], bm_ref.shape[2]
    tq = rows // 2
    n_diag = bm_ref.shape[0] - 1

    r = lax.broadcasted_iota(jnp.int32, (rows, kb), 0)
    col = lax.broadcasted_iota(jnp.int32, (rows, kb), 1)
    bias = jnp.where(r < tq, bias_ref[2 * p], bias_ref[2 * p + 1])
    bm_ref[0] = bias
    for dgn in range(n_diag):
        bm_ref[1 + dgn] = jnp.where(col < dgn * tq + lax.rem(r, tq), bias, MASKED_LOGIT)
    z_ref[...] = jnp.full(z_ref.shape, MASKED_LOGIT, F32)
    sp_ref[...] = jnp.zeros_like(sp_ref)
    d_ref[...] = jnp.zeros_like(d_ref)
    a_ref[...] = jnp.zeros_like(a_ref)
    acc_ref[...] = jnp.zeros_like(acc_ref)
    carry_ref[...] = jnp.zeros_like(carry_ref)
    lane = lax.broadcasted_iota(jnp.int32, (tq, HEAD_PAIR), 1)

    def logits(e, slot):
        q = q_ref[0, pl.ds(pl.multiple_of(tab_ref[0, e], tq), tq), :]
        zero = jnp.zeros_like(q)
        q2 = jnp.concatenate([jnp.where(lane < HEAD_DIM, q, zero), jnp.where(lane >= HEAD_DIM, q, zero)], axis=0)
        kblk = k_ref[0, pl.ds(pl.multiple_of(tab_ref[1, e], kb), kb), :]
        z_ref[slot] = _dot_nt(q2, kblk) + bm_ref[tab_ref[2, e]]

    def softplus(slot):
        z = z_ref[slot]
        sp = _softplus(z)
        z_ref[slot] = z - sp
        sp_ref[slot] = sp.astype(BF16)

    def block_cumsum(slot):
        d_ref[slot] = _dot(sp_ref[slot], t_ref[...])

    def attention_weights(e, slot):
        carry = jnp.where(tab_ref[3, e] == 1, 0.0, carry_ref[...])
        d = d_ref[slot]
        c = d + jnp.concatenate([carry] * (kb // LANES), axis=1)
        a_ref[slot] = jnp.exp(z_ref[slot] - c).astype(BF16)
        total = d[:, 0:1] + sp_ref[slot, :, 0:1].astype(F32)
        carry_ref[...] = carry + jnp.broadcast_to(total, carry_ref.shape)

    def weights_times_values(e, slot):
        vblk = v_ref[0, pl.ds(pl.multiple_of(tab_ref[1, e], kb), kb), :]
        acc = jnp.where(tab_ref[3, e] == 1, 0.0, acc_ref[...]) + _dot(a_ref[slot], vblk)
        acc_ref[...] = acc
        o_ref[0, pl.ds(pl.multiple_of(tab_ref[0, e], tq), tq), :] = jnp.where(
            lane < HEAD_DIM, acc[:tq], acc[tq:]).astype(o_ref.dtype)

    def body(it, _):
        for u in range(PIPE_SLOTS):
            e = it * PIPE_SLOTS + u
            slot = lambda s, u=u: (u + s) % PIPE_SLOTS
            weights_times_values(e, slot(0))
            attention_weights(e + 1, slot(1))
            block_cumsum(slot(2))
            softplus(slot(3))
            logits(e + 4, slot(4))
        return 0

    lax.fori_loop(0, n_steps // PIPE_SLOTS, body, 0)


def _attn_prompt(q, k, v, sb_bias):
    b, t, _ = q.shape
    tq = min(TQ, t)
    kb = min(KB, t)
    n_pairs = ATT_DIM // HEAD_PAIR
    tab, n_steps = _attn_schedule(t, tq, kb)
    seq_spec = pl.BlockSpec((1, t, HEAD_PAIR), lambda bi, p, tab: (bi, 0, p))
    grid_spec = pltpu.PrefetchScalarGridSpec(
        num_scalar_prefetch=1,
        grid=(b, n_pairs),
        in_specs=[pl.BlockSpec(memory_space=pltpu.SMEM), seq_spec, seq_spec, seq_spec,
                  pl.BlockSpec((kb, kb), lambda bi, p, tab: (0, 0))],
        out_specs=seq_spec,
        scratch_shapes=[pltpu.VMEM((PIPE_SLOTS, 2 * tq, kb), F32),
                        pltpu.VMEM((PIPE_SLOTS, 2 * tq, kb), BF16),
                        pltpu.VMEM((PIPE_SLOTS, 2 * tq, kb), F32),
                        pltpu.VMEM((PIPE_SLOTS, 2 * tq, kb), BF16),
                        pltpu.VMEM((1 + kb // tq, 2 * tq, kb), F32),
                        pltpu.VMEM((2 * tq, HEAD_PAIR), F32),
                        pltpu.VMEM((2 * tq, LANES), F32)],
    )
    return pl.pallas_call(
        functools.partial(_attn_prompt_body, n_steps),
        grid_spec=grid_spec,
        out_shape=jax.ShapeDtypeStruct((b, t, ATT_DIM), BF16),
        compiler_params=_params(("arbitrary", "arbitrary")),
        name="attn_prompt",
    )(jnp.asarray(tab), sb_bias, q, k, v, _tri(kb))


def _attn_sample_body(n_steps, pps, pt_ref, bias_ref, q_ref, kn_ref, vn_ref, *refs):
    del pt_ref
    k_refs = refs[:pps]
    v_refs = refs[pps:2 * pps]
    t_ref, o_ref, acc_ref, carry_ref, knew_ref, vnew_ref = refs[2 * pps:]
    j = pl.program_id(1)
    page = t_ref.shape[1]
    rows = N_HEADS * Q_PAD

    def blocks(loaders, masked):
        z = jnp.concatenate(
            [_dot(q_ref[0, h], load_k(h).astype(BF16)) + bias_ref[h]
             for load_k, _ in loaders for h in range(N_HEADS)], axis=0)
        if masked:
            r = lax.broadcasted_iota(jnp.int32, z.shape, 0)
            col = lax.broadcasted_iota(jnp.int32, z.shape, 1)
            z = jnp.where(col < lax.rem(r, Q_PAD), z, MASKED_LOGIT)
        sp = _softplus(z)
        own = z - sp
        sp = sp.astype(BF16)
        d = _dot(sp, t_ref[...])
        carry = carry_ref[...]
        for s, (_, load_v) in enumerate(loaders):
            blk = slice(s * rows, (s + 1) * rows)
            ab = jnp.exp(own[blk] - (d[blk] + carry)).astype(BF16)
            for h in range(N_HEADS):
                acc_ref[h] += _dot_nt(ab[h * Q_PAD:(h + 1) * Q_PAD], load_v(h).astype(BF16))
            total = d[blk, 0:1] + sp[blk, 0:1].astype(F32)
            carry = carry + jnp.broadcast_to(total, carry.shape)
        carry_ref[...] = carry

    @pl.when(j == 0)
    def _():
        acc_ref[...] = jnp.zeros_like(acc_ref)
        carry_ref[...] = jnp.zeros_like(carry_ref)
        knew_ref[...] = jnp.zeros_like(knew_ref)
        vnew_ref[...] = jnp.zeros_like(vnew_ref)
        n_new = kn_ref.shape[3]
        knew_ref[:, :, 0:n_new] = kn_ref[0]
        vnew_ref[:, :, 0:n_new] = vn_ref[0]
        blocks([(lambda h: knew_ref[h], lambda h: vnew_ref[h])], True)

    blocks([(lambda h, kr=kr: kr[h], lambda h, vr=vr: vr[h]) for kr, vr in zip(k_refs, v_refs)], False)

    @pl.when(j == n_steps - 1)
    def _():
        o_ref[0] = acc_ref[...]


def _attn_sample(q, k_new, v_new, cache_k, cache_v, page_table, sb_bias):
    n_seq = q.shape[0]
    page = cache_k.shape[4]
    n_pages = page_table.shape[1]
    pps = min(PAGES_PER_STEP, n_pages)
    n_steps = n_pages // pps

    def page_spec(s):
        return pl.BlockSpec((None, None, N_HEADS, HEAD_DIM, page),
                            lambda b, j, pt: (0, pt[b, n_pages - 1 - (j * pps + s)], 0, 0, 0))

    seq4 = lambda r, c: pl.BlockSpec((1, N_HEADS, r, c), lambda b, j, pt: (b, 0, 0, 0))
    n_new = k_new.shape[3]
    grid_spec = pltpu.PrefetchScalarGridSpec(
        num_scalar_prefetch=1,
        grid=(n_seq, n_steps),
        in_specs=[pl.BlockSpec(memory_space=pltpu.SMEM), seq4(Q_PAD, HEAD_DIM),
                  seq4(HEAD_DIM, n_new), seq4(HEAD_DIM, n_new)]
                 + [page_spec(s) for s in range(pps)] * 2
                 + [pl.BlockSpec((page, page), lambda b, j, pt: (0, 0))],
        out_specs=seq4(Q_PAD, HEAD_DIM),
        scratch_shapes=[pltpu.VMEM((N_HEADS, Q_PAD, HEAD_DIM), F32),
                        pltpu.VMEM((N_HEADS * Q_PAD, LANES), F32),
                        pltpu.VMEM((N_HEADS, HEAD_DIM, page), F32),
                        pltpu.VMEM((N_HEADS, HEAD_DIM, page), F32)],
    )
    return pl.pallas_call(
        functools.partial(_attn_sample_body, n_steps, pps),
        grid_spec=grid_spec,
        out_shape=jax.ShapeDtypeStruct((n_seq, N_HEADS, Q_PAD, HEAD_DIM), F32),
        compiler_params=_params(("arbitrary", "arbitrary")),
        name="attn_sample",
    )(page_table, sb_bias, q, k_new, v_new, *([cache_k] * pps), *([cache_v] * pps), _tri(page))


def _merge_body(x_ref, att_ref, ga_ref, cb_ref, wba_ref, wout_ref, g2_ref, wrh_ref, wrl_ref, rb_ref,
                h_ref, xn_ref, gate_ref):
    merged = ga_ref[...].astype(F32) * _dot(att_ref[...], wba_ref[...]) + cb_ref[...].astype(F32)
    h = x_ref[...] + _dot(merged.astype(BF16), wout_ref[...])
    h_ref[...] = h
    xn = _rms(h, g2_ref[...])
    xh, xl = _split_bf16(xn)
    xn_ref[...] = xh
    wrh = wrh_ref[...]
    logits = _dot(xh, wrh) + _dot(xl, wrh) + _dot(xh, wrl_ref[...])
    scores = _sigmoid(logits)
    work = scores + rb_ref[...]
    n_exp = work.shape[1]
    lane = lax.broadcasted_iota(jnp.int32, work.shape, 1)
    chosen = jnp.zeros(work.shape, jnp.bool_)
    for _ in range(TOP_K):
        m = jnp.max(work, axis=-1, keepdims=True)
        first = jnp.min(jnp.where(work == m, lane, n_exp), axis=-1, keepdims=True)
        sel = lane == first
        chosen = jnp.logical_or(chosen, sel)
        work = jnp.where(sel, -jnp.inf, work)
    picked = jnp.where(chosen, scores, 0.0)
    gate_ref[...] = picked / jnp.sum(picked, axis=-1, keepdims=True) * ROUTED_SCALE


def _merge(x, att, ga, cb, wba_bf, wout_bf, g2, wr_hi, wr_lo, rbias):
    n, d = x.shape
    tm = min(TM_PROJ, n)
    n_exp = wr_hi.shape[1]
    tok = lambda w: pl.BlockSpec((tm, w), lambda i: (i, 0))
    full = lambda shp: pl.BlockSpec(shp, lambda i: (0,) * len(shp))
    return pl.pallas_call(
        _merge_body,
        grid=(n // tm,),
        in_specs=[tok(d), tok(ATT_DIM), tok(d), tok(d), full((ATT_DIM, d)), full((d, d)), full((1, d)),
                  full((d, n_exp)), full((d, n_exp)), full((1, n_exp))],
        out_specs=(tok(d), tok(d), tok(n_exp)),
        out_shape=(jax.ShapeDtypeStruct((n, d), F32), jax.ShapeDtypeStruct((n, d), BF16),
                   jax.ShapeDtypeStruct((n, n_exp), F32)),
        compiler_params=_params(("arbitrary",)),
        name="merge_router",
    )(x, att, ga, cb, wba_bf, wout_bf, g2, wr_hi, wr_lo, rbias)


def _swiglu(x, wg, wu, wd):
    a = _dot(x, wg)
    hdn = (a * _sigmoid(a)) * _dot(x, wu)
    return _dot(hdn.astype(BF16), wd)


def _moe_body(xn_ref, h_ref, gate_ref, wg_ref, wu_ref, wd_ref, wgs_ref, wus_ref, wds_ref, gf_ref,
              y_ref, acc_ref):
    e = pl.program_id(1)
    x = xn_ref[...]

    @pl.when(e == 0)
    def _():
        acc_ref[...] = _swiglu(x, wgs_ref[...], wus_ref[...], wds_ref[...])

    gate = gate_ref[...]
    lane = lax.broadcasted_iota(jnp.int32, gate.shape, 1)
    g_e = jnp.sum(jnp.where(lane == e, gate, 0.0), axis=-1, keepdims=True)
    acc_ref[...] += g_e * _swiglu(x, wg_ref[0], wu_ref[0], wd_ref[0])

    @pl.when(e == pl.num_programs(1) - 1)
    def _():
        y_ref[...] = _rms(h_ref[...] + acc_ref[...], gf_ref[...])


def _moe(xn, h, gate, wg_bf, wu_bf, wd_bf, wgs_bf, wus_bf, wds_bf, gf):
    n, d = h.shape
    tm = min(TM_MOE, n)
    n_exp, _, de = wg_bf.shape
    ds = wgs_bf.shape[1]
    tok = lambda w: pl.BlockSpec((tm, w), lambda i, e: (i, 0))
    full = lambda shp: pl.BlockSpec(shp, lambda i, e: (0,) * len(shp))
    return pl.pallas_call(
        _moe_body,
        grid=(n // tm, n_exp),
        in_specs=[tok(d), tok(d), tok(n_exp),
                  pl.BlockSpec((1, d, de), lambda i, e: (e, 0, 0)),
                  pl.BlockSpec((1, d, de), lambda i, e: (e, 0, 0)),
                  pl.BlockSpec((1, de, d), lambda i, e: (e, 0, 0)),
                  full((d, ds)), full((d, ds)), full((ds, d)), full((1, d))],
        out_specs=tok(d),
        out_shape=jax.ShapeDtypeStruct((n, d), F32),
        scratch_shapes=[pltpu.VMEM((tm, d), F32)],
        compiler_params=_params(("arbitrary", "arbitrary")),
        name="moe",
    )(xn, h, gate, wg_bf, wu_bf, wd_bf, wgs_bf, wus_bf, wds_bf, gf)


def kernel(x_prompt, x_sample, cache_k, cache_v, state_conv, page_table, norm_mix_g, w_in, sb_bias, conv_w,
           w_branch_att, w_branch_conv, w_out, norm_ffn_g, w_router, router_bias, w_gate_e, w_up_e, w_down_e,
           w_gate_s, w_up_s, w_down_s, norm_final_g):
    depth = w_in.shape[0]
    assert depth == 1, "single trunk layer"
    b, t, d = x_prompt.shape
    db, ts, _ = x_sample.shape
    assert ts <= 8 and (CONV_DIM, ATT_DIM) == (state_conv.shape[-1], cache_k.shape[-2] * cache_k.shape[-1])
    l = 0
    row = lambda g: g.reshape(1, -1)
    w_in_bf = w_in[l].astype(BF16)
    wbc_bf = w_branch_conv[l].astype(BF16)
    wba_bf = w_branch_att[l].astype(BF16)
    wout_bf = w_out[l].astype(BF16)
    wr_hi = w_router[l].astype(BF16)
    wr_lo = (w_router[l] - wr_hi.astype(F32)).astype(BF16)
    moe_w = tuple(w[l].astype(BF16) for w in (w_gate_e, w_up_e, w_down_e, w_gate_s, w_up_s, w_down_s))
    bias = sb_bias[l]

    def channel_mixer(x2d, att, ga, cb):
        h, xn, gate = _merge(x2d, att, ga, cb, wba_bf, wout_bf, row(norm_ffn_g[l]), wr_hi, wr_lo,
                             row(router_bias[l]))
        return _moe(xn, h, gate, *moe_w, row(norm_final_g))

    q, k_bf, v_bf, k_f, v_f, ga, cb, conv_tail = _inproj_prompt(
        x_prompt, row(norm_mix_g[l]), w_in_bf, conv_w[l], wbc_bf)
    att = _attn_prompt(q, k_bf, v_bf, bias)
    n = b * t
    y_prompt = channel_mixer(x_prompt.reshape(n, d), att.reshape(n, ATT_DIM), ga.reshape(n, d),
                             cb.reshape(n, d)).reshape(b, t, d)

    ns = db * ts
    st = state_conv[l]
    n_hist = st.shape[1]
    hist1 = jnp.broadcast_to(st[:, n_hist - 1:n_hist], (db, ts, CONV_DIM)).reshape(ns, CONV_DIM)
    hist2 = jnp.concatenate([st, jnp.zeros((db, ts - n_hist, CONV_DIM), F32)], axis=1).reshape(ns, CONV_DIM)
    q_s, k_s, v_s, u_s, ga_s, cb_s = _inproj_sample(
        x_sample.reshape(ns, d), row(norm_mix_g[l]), w_in_bf, conv_w[l], wbc_bf, hist1, hist2, ts)

    by_head = lambda a: a.reshape(db, ts, N_HEADS, HEAD_DIM)
    q_h = jnp.pad(by_head(q_s).transpose(0, 2, 1, 3), ((0, 0), (0, 0), (0, Q_PAD - ts), (0, 0))).astype(BF16)
    dim_major = lambda c: c.transpose(0, 1, 3, 4, 2)
    att_s = _attn_sample(q_h, by_head(k_s).transpose(0, 2, 3, 1), by_head(v_s).transpose(0, 2, 3, 1),
                         dim_major(cache_k), dim_major(cache_v), page_table, bias)
    att_s = att_s[:, :, :ts].transpose(0, 2, 1, 3).reshape(ns, ATT_DIM).astype(BF16)
    y_sample = channel_mixer(x_sample.reshape(ns, d), att_s, ga_s, cb_s).reshape(db, ts, d)

    heads = lambda a, lead: a.reshape(1, *lead, N_HEADS, HEAD_DIM)
    return (y_prompt, y_sample,
            heads(k_f, (b, t)), heads(v_f, (b, t)), conv_tail[:, 6:8].reshape(1, b, 2, CONV_DIM),
            heads(k_s, (db, ts)), heads(v_s, (db, ts)),
            u_s.reshape(db, ts, CONV_DIM)[:, ts - 2:].reshape(1, db, 2, CONV_DIM))
```

```python
import functools

import jax
import jax.numpy as jnp
import numpy as np
from jax import lax
from jax.experimental import pallas as pl
from jax.experimental.pallas import tpu as pltpu

F32 = jnp.float32
BF16 = jnp.bfloat16

N_HEADS = 8
HEAD_DIM = 64
ATT_DIM = N_HEADS * HEAD_DIM
CONV_DIM = 512
TOP_K = 8
ROUTED_SCALE = 2.5
EPS = 1e-6

VMEM_LIMIT_BYTES = 56 * 1024 * 1024
LANES = 128
HEAD_PAIR = LANES

TM_PROJ = 512
TM_MOE = 1024
EXPERTS_PER_STEP = 4
TQ = 128
KB = 256
Q_PAD = 16
PAGES_PER_STEP = 8
PIPE = 4
PIPE_SLOTS = PIPE + 1
MASKED_LOGIT = -1e30
SOFTPLUS_CLAMP = 80.0


def _params(sem):
    return pltpu.CompilerParams(dimension_semantics=sem, vmem_limit_bytes=VMEM_LIMIT_BYTES)


def _rms(x, g):
    return (x * lax.rsqrt(jnp.mean(x * x, axis=-1, keepdims=True) + EPS)) * g


def _sigmoid(x):
    return 1.0 / (1.0 + jnp.exp(-x))


def _softplus(z):
    return jnp.maximum(z, jnp.log(1.0 + jnp.exp(jnp.minimum(z, SOFTPLUS_CLAMP))))


def _dot(a, b):
    return jnp.dot(a, b, preferred_element_type=F32)


def _dot_nt(a, b):
    return lax.dot_general(a, b, (((1,), (1,)), ((), ())), preferred_element_type=F32)


def _split_bf16(x):
    hi = x.astype(BF16)
    lo = (x - hi.astype(F32)).astype(BF16)
    return hi, lo


def _tri(n):
    j = lax.broadcasted_iota(jnp.int32, (n, n), 0)
    s = lax.broadcasted_iota(jnp.int32, (n, n), 1)
    return (j > s).astype(BF16)


def _project(xb, w_ref, lo, hi):
    return _dot(xb, w_ref[:, lo:hi])


def _conv_branch(xb, w_ref, cw_ref, wbc_ref, u, um1, um2):
    a = ATT_DIM
    cw = cw_ref[...]
    conv = _project(xb, w_ref, 3 * a, 3 * a + CONV_DIM) * (cw[0:1] * um2 + cw[1:2] * um1 + cw[2:3] * u)
    d = w_ref.shape[0]
    g_conv = _project(xb, w_ref, 3 * a + 3 * CONV_DIM + d, 3 * a + 3 * CONV_DIM + 2 * d)
    return _sigmoid(g_conv) * _dot(conv.astype(BF16), wbc_ref[...])


def _inproj_prompt_body(x_ref, g_ref, w_ref, cw_ref, wbc_ref,
                        q_ref, kb_ref, vb_ref, kf_ref, vf_ref, ga_ref, cb_ref, cs_ref, hist_ref):
    a, c, d = ATT_DIM, CONV_DIM, x_ref.shape[2]
    tm = x_ref.shape[1]
    xb = _rms(x_ref[0], g_ref[...]).astype(BF16)
    q_ref[0] = (_project(xb, w_ref, 0, a) * (HEAD_DIM ** -0.5)).astype(BF16)
    k = _project(xb, w_ref, a, 2 * a)
    kf_ref[0] = k
    kb_ref[0] = k.astype(BF16)
    v = _project(xb, w_ref, 2 * a, 3 * a)
    vf_ref[0] = v
    vb_ref[0] = v.astype(BF16)
    u = _project(xb, w_ref, 3 * a + c, 3 * a + 2 * c) * _project(xb, w_ref, 3 * a + 2 * c, 3 * a + 3 * c)

    @pl.when(pl.program_id(1) == 0)
    def _():
        hist_ref[...] = jnp.zeros_like(hist_ref)

    row = lax.broadcasted_iota(jnp.int32, u.shape, 0)
    h0 = hist_ref[6:7, :]
    h1 = hist_ref[7:8, :]
    um1 = jnp.where(row < 1, h1, pltpu.roll(u, 1, 0))
    um2 = jnp.where(row < 1, h0, jnp.where(row < 2, h1, pltpu.roll(u, 2, 0)))
    cb_ref[0] = _conv_branch(xb, w_ref, cw_ref, wbc_ref, u, um1, um2).astype(BF16)
    tail = u[tm - 8:tm, :]
    hist_ref[...] = tail
    cs_ref[0] = tail
    ga_ref[0] = _sigmoid(_project(xb, w_ref, 3 * a + 3 * c, 3 * a + 3 * c + d)).astype(BF16)


def _inproj_prompt(x, g, w_bf, conv_w, wbc_bf):
    b, t, d = x.shape
    tm = min(TM_PROJ, t)
    nt = t // tm
    n_in = w_bf.shape[1]
    tok = lambda n: pl.BlockSpec((1, tm, n), lambda i, j: (i, j, 0))
    full = lambda shp: pl.BlockSpec(shp, lambda i, j: (0,) * len(shp))
    out_shape = (
        jax.ShapeDtypeStruct((b, t, ATT_DIM), BF16),
        jax.ShapeDtypeStruct((b, t, ATT_DIM), BF16),
        jax.ShapeDtypeStruct((b, t, ATT_DIM), BF16),
        jax.ShapeDtypeStruct((b, t, ATT_DIM), F32),
        jax.ShapeDtypeStruct((b, t, ATT_DIM), F32),
        jax.ShapeDtypeStruct((b, t, d), BF16),
        jax.ShapeDtypeStruct((b, t, d), BF16),
        jax.ShapeDtypeStruct((b, 8, CONV_DIM), F32),
    )
    return pl.pallas_call(
        _inproj_prompt_body,
        grid=(b, nt),
        in_specs=[tok(d), full((1, d)), full((d, n_in)), full((3, CONV_DIM)), full((CONV_DIM, d))],
        out_specs=(tok(ATT_DIM), tok(ATT_DIM), tok(ATT_DIM), tok(ATT_DIM), tok(ATT_DIM), tok(d), tok(d),
                   pl.BlockSpec((1, 8, CONV_DIM), lambda i, j: (i, 0, 0))),
        out_shape=out_shape,
        scratch_shapes=[pltpu.VMEM((8, CONV_DIM), F32)],
        compiler_params=_params(("arbitrary", "arbitrary")),
        name="inproj_prompt",
    )(x, g, w_bf, conv_w, wbc_bf)


def _inproj_sample_body(ts, x_ref, g_ref, w_ref, cw_ref, wbc_ref, h1_ref, h2_ref,
                        q_ref, k_ref, v_ref, u_ref, ga_ref, cb_ref):
    a, c, d = ATT_DIM, CONV_DIM, x_ref.shape[1]
    xb = _rms(x_ref[...], g_ref[...]).astype(BF16)
    q_ref[...] = _project(xb, w_ref, 0, a) * (HEAD_DIM ** -0.5)
    k_ref[...] = _project(xb, w_ref, a, 2 * a)
    v_ref[...] = _project(xb, w_ref, 2 * a, 3 * a)
    u = _project(xb, w_ref, 3 * a + c, 3 * a + 2 * c) * _project(xb, w_ref, 3 * a + 2 * c, 3 * a + 3 * c)
    u_ref[...] = u
    tok = lax.rem(lax.broadcasted_iota(jnp.int32, u.shape, 0), ts)
    um1 = jnp.where(tok >= 1, pltpu.roll(u, 1, 0), h1_ref[...])
    um2 = jnp.where(tok >= 2, pltpu.roll(u, 2, 0), h2_ref[...])
    cb_ref[...] = _conv_branch(xb, w_ref, cw_ref, wbc_ref, u, um1, um2).astype(BF16)
    ga_ref[...] = _sigmoid(_project(xb, w_ref, 3 * a + 3 * c, 3 * a + 3 * c + d)).astype(BF16)


def _inproj_sample(x, g, w_bf, conv_w, wbc_bf, hist1, hist2, ts):
    n, d = x.shape
    out_shape = (
        jax.ShapeDtypeStruct((n, ATT_DIM), F32),
        jax.ShapeDtypeStruct((n, ATT_DIM), F32),
        jax.ShapeDtypeStruct((n, ATT_DIM), F32),
        jax.ShapeDtypeStruct((n, CONV_DIM), F32),
        jax.ShapeDtypeStruct((n, d), BF16),
        jax.ShapeDtypeStruct((n, d), BF16),
    )
    return pl.pallas_call(
        functools.partial(_inproj_sample_body, ts),
        out_shape=out_shape,
        compiler_params=pltpu.CompilerParams(vmem_limit_bytes=VMEM_LIMIT_BYTES),
        name="inproj_sample",
    )(x, g, w_bf, conv_w, wbc_bf, hist1, hist2)


def _attn_schedule(t, tq, kb):
    items = []
    for i in range(t // tq):
        jd = (i * tq) // kb
        for n in range(jd + 1):
            diag = 1 + ((i * tq) % kb) // tq if n == 0 else 0
            items.append((i * tq, (jd - n) * kb, diag, int(n == 0), int(n == jd)))
    idle = (0, 0, 0, 1, 0)
    n_steps = -(-(len(items) + PIPE) // PIPE_SLOTS) * PIPE_SLOTS
    assert items[0][3] == 1 and items[0][4] == 1
    items = [idle] * PIPE + items + [items[0]] * (n_steps - len(items))
    return np.asarray(items, np.int32).T, n_steps


def _attn_prompt_body(n_steps, tab_ref, bias_ref, q_ref, k_ref, v_ref, t_ref, o_ref,
                      z_ref, sp_ref, d_ref, a_ref, bm_ref, acc_ref, carry_ref):
    p = pl.program_id(1)
    rows, kb = bm_ref.shape[1], bm_ref.shape[2]
    tq = rows // 2
    n_diag = bm_ref.shape[0] - 1

    r = lax.broadcasted_iota(jnp.int32, (rows, kb), 0)
    col = lax.broadcasted_iota(jnp.int32, (rows, kb), 1)
    bias = jnp.where(r < tq, bias_ref[2 * p], bias_ref[2 * p + 1])
    bm_ref[0] = bias
    for dgn in range(n_diag):
        bm_ref[1 + dgn] = jnp.where(col < dgn * tq + lax.rem(r, tq), bias, MASKED_LOGIT)
    z_ref[...] = jnp.full(z_ref.shape, MASKED_LOGIT, F32)
    sp_ref[...] = jnp.zeros_like(sp_ref)
    d_ref[...] = jnp.zeros_like(d_ref)
    a_ref[...] = jnp.zeros_like(a_ref)
    acc_ref[...] = jnp.zeros_like(acc_ref)
    carry_ref[...] = jnp.zeros_like(carry_ref)
    lane = lax.broadcasted_iota(jnp.int32, (tq, HEAD_PAIR), 1)

    def logits(e, slot):
        q = q_ref[0, pl.ds(pl.multiple_of(tab_ref[0, e], tq), tq), :]
        zero = jnp.zeros_like(q)
        q2 = jnp.concatenate([jnp.where(lane < HEAD_DIM, q, zero), jnp.where(lane >= HEAD_DIM, q, zero)], axis=0)
        kblk = k_ref[0, pl.ds(pl.multiple_of(tab_ref[1, e], kb), kb), :]
        z_ref[slot] = _dot_nt(q2, kblk) + bm_ref[tab_ref[2, e]]

    def softplus(slot):
        z = z_ref[slot]
        sp = _softplus(z)
        z_ref[slot] = z - sp
        sp_ref[slot] = sp.astype(BF16)

    def block_cumsum(slot):
        d_ref[slot] = _dot(sp_ref[slot], t_ref[...])

    def attention_weights(e, slot):
        carry = jnp.where(tab_ref[3, e] == 1, 0.0, carry_ref[...])
        d = d_ref[slot]
        c = d + jnp.concatenate([carry] * (kb // LANES), axis=1)
        a_ref[slot] = jnp.exp(z_ref[slot] - c).astype(BF16)
        total = d[:, 0:1] + sp_ref[slot, :, 0:1].astype(F32)
        carry_ref[...] = carry + jnp.broadcast_to(total, carry_ref.shape)

    def weights_times_values(e, slot):
        vblk = v_ref[0, pl.ds(pl.multiple_of(tab_ref[1, e], kb), kb), :]
        acc = jnp.where(tab_ref[3, e] == 1, 0.0, acc_ref[...]) + _dot(a_ref[slot], vblk)
        acc_ref[...] = acc
        o_ref[0, pl.ds(pl.multiple_of(tab_ref[0, e], tq), tq), :] = jnp.where(
            lane < HEAD_DIM, acc[:tq], acc[tq:]).astype(o_ref.dtype)

    def body(it, _):
        for u in range(PIPE_SLOTS):
            e = it * PIPE_SLOTS + u
            slot = lambda s, u=u: (u + s) % PIPE_SLOTS
            weights_times_values(e, slot(0))
            attention_weights(e + 1, slot(1))
            block_cumsum(slot(2))
            softplus(slot(3))
            logits(e + 4, slot(4))
        return 0

    lax.fori_loop(0, n_steps // PIPE_SLOTS, body, 0)


def _attn_prompt(q, k, v, sb_bias):
    b, t, _ = q.shape
    tq = min(TQ, t)
    kb = min(KB, t)
    n_pairs = ATT_DIM // HEAD_PAIR
    tab, n_steps = _attn_schedule(t, tq, kb)
    seq_spec = pl.BlockSpec((1, t, HEAD_PAIR), lambda bi, p, tab: (bi, 0, p))
    grid_spec = pltpu.PrefetchScalarGridSpec(
        num_scalar_prefetch=1,
        grid=(b, n_pairs),
        in_specs=[pl.BlockSpec(memory_space=pltpu.SMEM), seq_spec, seq_spec, seq_spec,
                  pl.BlockSpec((kb, kb), lambda bi, p, tab: (0, 0))],
        out_specs=seq_spec,
        scratch_shapes=[pltpu.VMEM((PIPE_SLOTS, 2 * tq, kb), F32),
                        pltpu.VMEM((PIPE_SLOTS, 2 * tq, kb), BF16),
                        pltpu.VMEM((PIPE_SLOTS, 2 * tq, kb), F32),
                        pltpu.VMEM((PIPE_SLOTS, 2 * tq, kb), BF16),
                        pltpu.VMEM((1 + kb // tq, 2 * tq, kb), F32),
                        pltpu.VMEM((2 * tq, HEAD_PAIR), F32),
                        pltpu.VMEM((2 * tq, LANES), F32)],
    )
    return pl.pallas_call(
        functools.partial(_attn_prompt_body, n_steps),
        grid_spec=grid_spec,
        out_shape=jax.ShapeDtypeStruct((b, t, ATT_DIM), BF16),
        compiler_params=_params(("arbitrary", "arbitrary")),
        name="attn_prompt",
    )(jnp.asarray(tab), sb_bias, q, k, v, _tri(kb))


def _attn_sample_body(n_steps, pps, pt_ref, bias_ref, q_ref, kn_ref, vn_ref, *refs):
    del pt_ref
    k_refs = refs[:pps]
    v_refs = refs[pps:2 * pps]
    t_ref, o_ref, acc_ref, carry_ref, knew_ref, vnew_ref = refs[2 * pps:]
    j = pl.program_id(1)
    page = t_ref.shape[1]
    rows = N_HEADS * Q_PAD

    def blocks(loaders, masked):
        z = jnp.concatenate(
            [_dot(q_ref[0, h], load_k(h).astype(BF16)) + bias_ref[h]
             for load_k, _ in loaders for h in range(N_HEADS)], axis=0)
        if masked:
            r = lax.broadcasted_iota(jnp.int32, z.shape, 0)
            col = lax.broadcasted_iota(jnp.int32, z.shape, 1)
            z = jnp.where(col < lax.rem(r, Q_PAD), z, MASKED_LOGIT)
        sp = _softplus(z)
        own = z - sp
        sp = sp.astype(BF16)
        d = _dot(sp, t_ref[...])
        carry = carry_ref[...]
        for s, (_, load_v) in enumerate(loaders):
            blk = slice(s * rows, (s + 1) * rows)
            ab = jnp.exp(own[blk] - (d[blk] + carry)).astype(BF16)
            for h in range(N_HEADS):
                acc_ref[h] += _dot_nt(ab[h * Q_PAD:(h + 1) * Q_PAD], load_v(h).astype(BF16))
            total = d[blk, 0:1] + sp[blk, 0:1].astype(F32)
            carry = carry + jnp.broadcast_to(total, carry.shape)
        carry_ref[...] = carry

    @pl.when(j == 0)
    def _():
        acc_ref[...] = jnp.zeros_like(acc_ref)
        carry_ref[...] = jnp.zeros_like(carry_ref)
        knew_ref[...] = jnp.zeros_like(knew_ref)
        vnew_ref[...] = jnp.zeros_like(vnew_ref)
        n_new = kn_ref.shape[3]
        knew_ref[:, :, 0:n_new] = kn_ref[0]
        vnew_ref[:, :, 0:n_new] = vn_ref[0]
        blocks([(lambda h: knew_ref[h], lambda h: vnew_ref[h])], True)

    blocks([(lambda h, kr=kr: kr[h], lambda h, vr=vr: vr[h]) for kr, vr in zip(k_refs, v_refs)], False)

    @pl.when(j == n_steps - 1)
    def _():
        o_ref[0] = acc_ref[...]


def _attn_sample(q, k_new, v_new, cache_k, cache_v, page_table, sb_bias):
    n_seq = q.shape[0]
    page = cache_k.shape[4]
    n_pages = page_table.shape[1]
    pps = min(PAGES_PER_STEP, n_pages)
    n_steps = n_pages // pps

    def page_spec(s):
        return pl.BlockSpec((None, None, N_HEADS, HEAD_DIM, page),
                            lambda b, j, pt: (0, pt[b, n_pages - 1 - (j * pps + s)], 0, 0, 0))

    seq4 = lambda r, c: pl.BlockSpec((1, N_HEADS, r, c), lambda b, j, pt: (b, 0, 0, 0))
    n_new = k_new.shape[3]
    grid_spec = pltpu.PrefetchScalarGridSpec(
        num_scalar_prefetch=1,
        grid=(n_seq, n_steps),
        in_specs=[pl.BlockSpec(memory_space=pltpu.SMEM), seq4(Q_PAD, HEAD_DIM),
                  seq4(HEAD_DIM, n_new), seq4(HEAD_DIM, n_new)]
                 + [page_spec(s) for s in range(pps)] * 2
                 + [pl.BlockSpec((page, page), lambda b, j, pt: (0, 0))],
        out_specs=seq4(Q_PAD, HEAD_DIM),
        scratch_shapes=[pltpu.VMEM((N_HEADS, Q_PAD, HEAD_DIM), F32),
                        pltpu.VMEM((N_HEADS * Q_PAD, LANES), F32),
                        pltpu.VMEM((N_HEADS, HEAD_DIM, page), F32),
                        pltpu.VMEM((N_HEADS, HEAD_DIM, page), F32)],
    )
    return pl.pallas_call(
        functools.partial(_attn_sample_body, n_steps, pps),
        grid_spec=grid_spec,
        out_shape=jax.ShapeDtypeStruct((n_seq, N_HEADS, Q_PAD, HEAD_DIM), F32),
        compiler_params=_params(("arbitrary", "arbitrary")),
        name="attn_sample",
    )(page_table, sb_bias, q, k_new, v_new, *([cache_k] * pps), *([cache_v] * pps), _tri(page))


def _merge_body(x_ref, att_ref, ga_ref, cb_ref, wba_ref, wout_ref, g2_ref, wrh_ref, wrl_ref, rb_ref,
                h_ref, xn_ref, gate_ref):
    merged = ga_ref[...].astype(F32) * _dot(att_ref[...], wba_ref[...]) + cb_ref[...].astype(F32)
    h = x_ref[...] + _dot(merged.astype(BF16), wout_ref[...])
    h_ref[...] = h
    xn = _rms(h, g2_ref[...])
    xh, xl = _split_bf16(xn)
    xn_ref[...] = xh
    wrh = wrh_ref[...]
    logits = _dot(xh, wrh) + _dot(xl, wrh) + _dot(xh, wrl_ref[...])
    scores = _sigmoid(logits)
    work = scores + rb_ref[...]
    n_exp = work.shape[1]
    lane = lax.broadcasted_iota(jnp.int32, work.shape, 1)
    chosen = jnp.zeros(work.shape, jnp.bool_)
    for _ in range(TOP_K):
        m = jnp.max(work, axis=-1, keepdims=True)
        first = jnp.min(jnp.where(work == m, lane, n_exp), axis=-1, keepdims=True)
        sel = lane == first
        chosen = jnp.logical_or(chosen, sel)
        work = jnp.where(sel, -jnp.inf, work)
    picked = jnp.where(chosen, scores, 0.0)
    gate_ref[...] = picked / jnp.sum(picked, axis=-1, keepdims=True) * ROUTED_SCALE


def _merge(x, att, ga, cb, wba_bf, wout_bf, g2, wr_hi, wr_lo, rbias):
    n, d = x.shape
    tm = min(TM_PROJ, n)
    n_exp = wr_hi.shape[1]
    tok = lambda w: pl.BlockSpec((tm, w), lambda i: (i, 0))
    full = lambda shp: pl.BlockSpec(shp, lambda i: (0,) * len(shp))
    return pl.pallas_call(
        _merge_body,
        grid=(n // tm,),
        in_specs=[tok(d), tok(ATT_DIM), tok(d), tok(d), full((ATT_DIM, d)), full((d, d)), full((1, d)),
                  full((d, n_exp)), full((d, n_exp)), full((1, n_exp))],
        out_specs=(tok(d), tok(d), tok(n_exp)),
        out_shape=(jax.ShapeDtypeStruct((n, d), F32), jax.ShapeDtypeStruct((n, d), BF16),
                   jax.ShapeDtypeStruct((n, n_exp), F32)),
        compiler_params=_params(("arbitrary",)),
        name="merge_router",
    )(x, att, ga, cb, wba_bf, wout_bf, g2, wr_hi, wr_lo, rbias)


def _swiglu_hidden(x, wg, wu):
    a = _dot(x, wg)
    return (a * _sigmoid(a)) * _dot(x, wu)


def _moe_body(xn_ref, h_ref, gate_ref, wg_ref, wu_ref, wd_ref, wgs_ref, wus_ref, wds_ref, gf_ref, y_ref):
    s = pl.program_id(1)
    x = xn_ref[...]

    @pl.when(s == 0)
    def _():
        y_ref[...] = _dot(_swiglu_hidden(x, wgs_ref[...], wus_ref[...]).astype(BF16), wds_ref[...])

    gate = gate_ref[...]
    lane = lax.broadcasted_iota(jnp.int32, gate.shape, 1)
    hidden = []
    for j in range(wg_ref.shape[0]):
        g_e = jnp.sum(jnp.where(lane == s * wg_ref.shape[0] + j, gate, 0.0), axis=-1, keepdims=True)
        hidden.append((g_e * _swiglu_hidden(x, wg_ref[j], wu_ref[j])).astype(BF16))
    y_ref[...] += _dot(jnp.concatenate(hidden, axis=1), wd_ref[...])

    @pl.when(s == pl.num_programs(1) - 1)
    def _():
        y_ref[...] = _rms(h_ref[...] + y_ref[...], gf_ref[...])


def _token_tile(n, largest):
    for tm in range(largest - largest % 16, 0, -16):
        if n % tm == 0:
            return tm
    return n


def _moe(xn, h, gate, wg_bf, wu_bf, wd_bf, wgs_bf, wus_bf, wds_bf, gf):
    n, d = h.shape
    tm = _token_tile(n, TM_MOE)
    n_exp, _, de = wg_bf.shape
    per = EXPERTS_PER_STEP
    assert n_exp % per == 0
    ds = wgs_bf.shape[1]
    tok = lambda w: pl.BlockSpec((tm, w), lambda i, s: (i, 0))
    full = lambda shp: pl.BlockSpec(shp, lambda i, s: (0,) * len(shp))
    return pl.pallas_call(
        _moe_body,
        grid=(n // tm, n_exp // per),
        in_specs=[tok(d), tok(d), tok(n_exp),
                  pl.BlockSpec((per, d, de), lambda i, s: (s, 0, 0)),
                  pl.BlockSpec((per, d, de), lambda i, s: (s, 0, 0)),
                  pl.BlockSpec((per * de, d), lambda i, s: (s, 0)),
                  full((d, ds)), full((d, ds)), full((ds, d)), full((1, d))],
        out_specs=tok(d),
        out_shape=jax.ShapeDtypeStruct((n, d), F32),
        compiler_params=_params(("arbitrary", "arbitrary")),
        name="moe",
    )(xn, h, gate, wg_bf, wu_bf, wd_bf.reshape(n_exp * de, d), wgs_bf, wus_bf, wds_bf, gf)


def kernel(x_prompt, x_sample, cache_k, cache_v, state_conv, page_table, norm_mix_g, w_in, sb_bias, conv_w,
           w_branch_att, w_branch_conv, w_out, norm_ffn_g, w_router, router_bias, w_gate_e, w_up_e, w_down_e,
           w_gate_s, w_up_s, w_down_s, norm_final_g):
    depth = w_in.shape[0]
    assert depth == 1, "single trunk layer"
    b, t, d = x_prompt.shape
    db, ts, _ = x_sample.shape
    assert ts <= 8 and (CONV_DIM, ATT_DIM) == (state_conv.shape[-1], cache_k.shape[-2] * cache_k.shape[-1])
    l = 0
    row = lambda g: g.reshape(1, -1)
    w_in_bf = w_in[l].astype(BF16)
    wbc_bf = w_branch_conv[l].astype(BF16)
    wba_bf = w_branch_att[l].astype(BF16)
    wout_bf = w_out[l].astype(BF16)
    wr_hi = w_router[l].astype(BF16)
    wr_lo = (w_router[l] - wr_hi.astype(F32)).astype(BF16)
    moe_w = tuple(w[l].astype(BF16) for w in (w_gate_e, w_up_e, w_down_e, w_gate_s, w_up_s, w_down_s))
    bias = sb_bias[l]

    def channel_mixer(x2d, att, ga, cb):
        h, xn, gate = _merge(x2d, att, ga, cb, wba_bf, wout_bf, row(norm_ffn_g[l]), wr_hi, wr_lo,
                             row(router_bias[l]))
        return _moe(xn, h, gate, *moe_w, row(norm_final_g))

    q, k_bf, v_bf, k_f, v_f, ga, cb, conv_tail = _inproj_prompt(
        x_prompt, row(norm_mix_g[l]), w_in_bf, conv_w[l], wbc_bf)
    att = _attn_prompt(q, k_bf, v_bf, bias)
    n = b * t
    y_prompt = channel_mixer(x_prompt.reshape(n, d), att.reshape(n, ATT_DIM), ga.reshape(n, d),
                             cb.reshape(n, d)).reshape(b, t, d)

    ns = db * ts
    st = state_conv[l]
    n_hist = st.shape[1]
    hist1 = jnp.broadcast_to(st[:, n_hist - 1:n_hist], (db, ts, CONV_DIM)).reshape(ns, CONV_DIM)
    hist2 = jnp.concatenate([st, jnp.zeros((db, ts - n_hist, CONV_DIM), F32)], axis=1).reshape(ns, CONV_DIM)
    q_s, k_s, v_s, u_s, ga_s, cb_s = _inproj_sample(
        x_sample.reshape(ns, d), row(norm_mix_g[l]), w_in_bf, conv_w[l], wbc_bf, hist1, hist2, ts)

    by_head = lambda a: a.reshape(db, ts, N_HEADS, HEAD_DIM)
    q_h = jnp.pad(by_head(q_s).transpose(0, 2, 1, 3), ((0, 0), (0, 0), (0, Q_PAD - ts), (0, 0))).astype(BF16)
    dim_major = lambda c: c.transpose(0, 1, 3, 4, 2)
    att_s = _attn_sample(q_h, by_head(k_s).transpose(0, 2, 3, 1), by_head(v_s).transpose(0, 2, 3, 1),
                         dim_major(cache_k), dim_major(cache_v), page_table, bias)
    att_s = att_s[:, :, :ts].transpose(0, 2, 1, 3).reshape(ns, ATT_DIM).astype(BF16)
    y_sample = channel_mixer(x_sample.reshape(ns, d), att_s, ga_s, cb_s).reshape(db, ts, d)

    heads = lambda a, lead: a.reshape(1, *lead, N_HEADS, HEAD_DIM)
    return (y_prompt, y_sample,
            heads(k_f, (b, t)), heads(v_f, (b, t)), conv_tail[:, 6:8].reshape(1, b, 2, CONV_DIM),
            heads(k_s, (db, ts)), heads(v_s, (db, ts)),
            u_s.reshape(db, ts, CONV_DIM)[:, ts - 2:].reshape(1, db, 2, CONV_DIM))
```

```python
import functools

import jax
import jax.numpy as jnp
import numpy as np
from jax import lax
from jax.experimental import pallas as pl
from jax.experimental.pallas import tpu as pltpu

F32 = jnp.float32
BF16 = jnp.bfloat16

N_HEADS = 8
HEAD_DIM = 64
ATT_DIM = N_HEADS * HEAD_DIM
CONV_DIM = 512
TOP_K = 8
ROUTED_SCALE = 2.5
EPS = 1e-6

VMEM_LIMIT_BYTES = 56 * 1024 * 1024
LANES = 128
HEAD_PAIR = LANES

TM_PROJ = 512
TM_MOE = 1024
EXPERTS_PER_STEP = 8
TQ = 128
KB = 256
Q_PAD = 16
PAGES_PER_STEP = 8
PIPE = 4
PIPE_SLOTS = PIPE + 1
MASKED_LOGIT = -1e30
SOFTPLUS_CLAMP = 80.0


def _params(sem):
    return pltpu.CompilerParams(dimension_semantics=sem, vmem_limit_bytes=VMEM_LIMIT_BYTES)


def _rms(x, g):
    return (x * lax.rsqrt(jnp.mean(x * x, axis=-1, keepdims=True) + EPS)) * g


def _sigmoid(x):
    return 1.0 / (1.0 + jnp.exp(-x))


def _softplus(z):
    return jnp.maximum(z, jnp.log(1.0 + jnp.exp(jnp.minimum(z, SOFTPLUS_CLAMP))))


def _dot(a, b):
    return jnp.dot(a, b, preferred_element_type=F32)


def _dot_nt(a, b):
    return lax.dot_general(a, b, (((1,), (1,)), ((), ())), preferred_element_type=F32)


def _split_bf16(x):
    hi = x.astype(BF16)
    lo = (x - hi.astype(F32)).astype(BF16)
    return hi, lo


def _tri(n):
    j = lax.broadcasted_iota(jnp.int32, (n, n), 0)
    s = lax.broadcasted_iota(jnp.int32, (n, n), 1)
    return (j > s).astype(BF16)


def _project(xb, w_ref, lo, hi):
    return _dot(xb, w_ref[:, lo:hi])


def _conv_branch(xb, w_ref, cw_ref, wbc_ref, u, um1, um2):
    a = ATT_DIM
    cw = cw_ref[...]
    conv = _project(xb, w_ref, 3 * a, 3 * a + CONV_DIM) * (cw[0:1] * um2 + cw[1:2] * um1 + cw[2:3] * u)
    d = w_ref.shape[0]
    g_conv = _project(xb, w_ref, 3 * a + 3 * CONV_DIM + d, 3 * a + 3 * CONV_DIM + 2 * d)
    return _sigmoid(g_conv) * _dot(conv.astype(BF16), wbc_ref[...])


def _inproj_prompt_body(x_ref, g_ref, w_ref, cw_ref, wbc_ref,
                        q_ref, kb_ref, vb_ref, kf_ref, vf_ref, ga_ref, cb_ref, cs_ref, hist_ref):
    a, c, d = ATT_DIM, CONV_DIM, x_ref.shape[2]
    tm = x_ref.shape[1]
    xb = _rms(x_ref[0], g_ref[...]).astype(BF16)
    q_ref[0] = (_project(xb, w_ref, 0, a) * (HEAD_DIM ** -0.5)).astype(BF16)
    k = _project(xb, w_ref, a, 2 * a)
    kf_ref[0] = k
    kb_ref[0] = k.astype(BF16)
    v = _project(xb, w_ref, 2 * a, 3 * a)
    vf_ref[0] = v
    vb_ref[0] = v.astype(BF16)
    u = _project(xb, w_ref, 3 * a + c, 3 * a + 2 * c) * _project(xb, w_ref, 3 * a + 2 * c, 3 * a + 3 * c)

    @pl.when(pl.program_id(1) == 0)
    def _():
        hist_ref[...] = jnp.zeros_like(hist_ref)

    row = lax.broadcasted_iota(jnp.int32, u.shape, 0)
    h0 = hist_ref[6:7, :]
    h1 = hist_ref[7:8, :]
    um1 = jnp.where(row < 1, h1, pltpu.roll(u, 1, 0))
    um2 = jnp.where(row < 1, h0, jnp.where(row < 2, h1, pltpu.roll(u, 2, 0)))
    cb_ref[0] = _conv_branch(xb, w_ref, cw_ref, wbc_ref, u, um1, um2).astype(BF16)
    tail = u[tm - 8:tm, :]
    hist_ref[...] = tail
    cs_ref[0] = tail
    ga_ref[0] = _sigmoid(_project(xb, w_ref, 3 * a + 3 * c, 3 * a + 3 * c + d)).astype(BF16)


def _inproj_prompt(x, g, w_bf, conv_w, wbc_bf):
    b, t, d = x.shape
    tm = min(TM_PROJ, t)
    nt = t // tm
    n_in = w_bf.shape[1]
    tok = lambda n: pl.BlockSpec((1, tm, n), lambda i, j: (i, j, 0))
    full = lambda shp: pl.BlockSpec(shp, lambda i, j: (0,) * len(shp))
    out_shape = (
        jax.ShapeDtypeStruct((b, t, ATT_DIM), BF16),
        jax.ShapeDtypeStruct((b, t, ATT_DIM), BF16),
        jax.ShapeDtypeStruct((b, t, ATT_DIM), BF16),
        jax.ShapeDtypeStruct((b, t, ATT_DIM), F32),
        jax.ShapeDtypeStruct((b, t, ATT_DIM), F32),
        jax.ShapeDtypeStruct((b, t, d), BF16),
        jax.ShapeDtypeStruct((b, t, d), BF16),
        jax.ShapeDtypeStruct((b, 8, CONV_DIM), F32),
    )
    return pl.pallas_call(
        _inproj_prompt_body,
        grid=(b, nt),
        in_specs=[tok(d), full((1, d)), full((d, n_in)), full((3, CONV_DIM)), full((CONV_DIM, d))],
        out_specs=(tok(ATT_DIM), tok(ATT_DIM), tok(ATT_DIM), tok(ATT_DIM), tok(ATT_DIM), tok(d), tok(d),
                   pl.BlockSpec((1, 8, CONV_DIM), lambda i, j: (i, 0, 0))),
        out_shape=out_shape,
        scratch_shapes=[pltpu.VMEM((8, CONV_DIM), F32)],
        compiler_params=_params(("arbitrary", "arbitrary")),
        name="inproj_prompt",
    )(x, g, w_bf, conv_w, wbc_bf)


def _inproj_sample_body(ts, x_ref, g_ref, w_ref, cw_ref, wbc_ref, h1_ref, h2_ref,
                        q_ref, k_ref, v_ref, u_ref, ga_ref, cb_ref):
    a, c, d = ATT_DIM, CONV_DIM, x_ref.shape[1]
    xb = _rms(x_ref[...], g_ref[...]).astype(BF16)
    q_ref[...] = _project(xb, w_ref, 0, a) * (HEAD_DIM ** -0.5)
    k_ref[...] = _project(xb, w_ref, a, 2 * a)
    v_ref[...] = _project(xb, w_ref, 2 * a, 3 * a)
    u = _project(xb, w_ref, 3 * a + c, 3 * a + 2 * c) * _project(xb, w_ref, 3 * a + 2 * c, 3 * a + 3 * c)
    u_ref[...] = u
    tok = lax.rem(lax.broadcasted_iota(jnp.int32, u.shape, 0), ts)
    um1 = jnp.where(tok >= 1, pltpu.roll(u, 1, 0), h1_ref[...])
    um2 = jnp.where(tok >= 2, pltpu.roll(u, 2, 0), h2_ref[...])
    cb_ref[...] = _conv_branch(xb, w_ref, cw_ref, wbc_ref, u, um1, um2).astype(BF16)
    ga_ref[...] = _sigmoid(_project(xb, w_ref, 3 * a + 3 * c, 3 * a + 3 * c + d)).astype(BF16)


def _inproj_sample(x, g, w_bf, conv_w, wbc_bf, hist1, hist2, ts):
    n, d = x.shape
    out_shape = (
        jax.ShapeDtypeStruct((n, ATT_DIM), F32),
        jax.ShapeDtypeStruct((n, ATT_DIM), F32),
        jax.ShapeDtypeStruct((n, ATT_DIM), F32),
        jax.ShapeDtypeStruct((n, CONV_DIM), F32),
        jax.ShapeDtypeStruct((n, d), BF16),
        jax.ShapeDtypeStruct((n, d), BF16),
    )
    return pl.pallas_call(
        functools.partial(_inproj_sample_body, ts),
        out_shape=out_shape,
        compiler_params=pltpu.CompilerParams(vmem_limit_bytes=VMEM_LIMIT_BYTES),
        name="inproj_sample",
    )(x, g, w_bf, conv_w, wbc_bf, hist1, hist2)


def _attn_schedule(t, tq, kb):
    items = []
    for i in range(t // tq):
        jd = (i * tq) // kb
        for n in range(jd + 1):
            diag = 1 + ((i * tq) % kb) // tq if n == 0 else 0
            items.append((i * tq, (jd - n) * kb, diag, int(n == 0), int(n == jd)))
    idle = (0, 0, 0, 1, 0)
    n_steps = -(-(len(items) + PIPE) // PIPE_SLOTS) * PIPE_SLOTS
    assert items[0][3] == 1 and items[0][4] == 1
    items = [idle] * PIPE + items + [items[0]] * (n_steps - len(items))
    return np.asarray(items, np.int32).T, n_steps


def _attn_prompt_body(n_steps, tab_ref, bias_ref, q_ref, k_ref, v_ref, t_ref, o_ref,
                      z_ref, sp_ref, d_ref, a_ref, bm_ref, acc_ref, carry_ref):
    p = pl.program_id(1)
    rows, kb = bm_ref.shape[1], bm_ref.shape[2]
    tq = rows // 2
    n_diag = bm_ref.shape[0] - 1

    r = lax.broadcasted_iota(jnp.int32, (rows, kb), 0)
    col = lax.broadcasted_iota(jnp.int32, (rows, kb), 1)
    bias = jnp.where(r < tq, bias_ref[2 * p], bias_ref[2 * p + 1])
    bm_ref[0] = bias
    for dgn in range(n_diag):
        bm_ref[1 + dgn] = jnp.where(col < dgn * tq + lax.rem(r, tq), bias, MASKED_LOGIT)
    z_ref[...] = jnp.full(z_ref.shape, MASKED_LOGIT, F32)
    sp_ref[...] = jnp.zeros_like(sp_ref)
    d_ref[...] = jnp.zeros_like(d_ref)
    a_ref[...] = jnp.zeros_like(a_ref)
    acc_ref[...] = jnp.zeros_like(acc_ref)
    carry_ref[...] = jnp.zeros_like(carry_ref)
    lane = lax.broadcasted_iota(jnp.int32, (tq, HEAD_PAIR), 1)

    def logits(e, slot):
        q = q_ref[0, pl.ds(pl.multiple_of(tab_ref[0, e], tq), tq), :]
        zero = jnp.zeros_like(q)
        q2 = jnp.concatenate([jnp.where(lane < HEAD_DIM, q, zero), jnp.where(lane >= HEAD_DIM, q, zero)], axis=0)
        kblk = k_ref[0, pl.ds(pl.multiple_of(tab_ref[1, e], kb), kb), :]
        z_ref[slot] = _dot_nt(q2, kblk) + bm_ref[tab_ref[2, e]]

    def softplus(slot):
        z = z_ref[slot]
        sp = _softplus(z)
        z_ref[slot] = z - sp
        sp_ref[slot] = sp.astype(BF16)

    def block_cumsum(slot):
        d_ref[slot] = _dot(sp_ref[slot], t_ref[...])

    def attention_weights(e, slot):
        carry = jnp.where(tab_ref[3, e] == 1, 0.0, carry_ref[...])
        d = d_ref[slot]
        c = d + jnp.concatenate([carry] * (kb // LANES), axis=1)
        a_ref[slot] = jnp.exp(z_ref[slot] - c).astype(BF16)
        total = d[:, 0:1] + sp_ref[slot, :, 0:1].astype(F32)
        carry_ref[...] = carry + jnp.broadcast_to(total, carry_ref.shape)

    def weights_times_values(e, slot):
        vblk = v_ref[0, pl.ds(pl.multiple_of(tab_ref[1, e], kb), kb), :]
        acc = jnp.where(tab_ref[3, e] == 1, 0.0, acc_ref[...]) + _dot(a_ref[slot], vblk)
        acc_ref[...] = acc
        o_ref[0, pl.ds(pl.multiple_of(tab_ref[0, e], tq), tq), :] = jnp.where(
            lane < HEAD_DIM, acc[:tq], acc[tq:]).astype(o_ref.dtype)

    def body(it, _):
        for u in range(PIPE_SLOTS):
            e = it * PIPE_SLOTS + u
            slot = lambda s, u=u: (u + s) % PIPE_SLOTS
            weights_times_values(e, slot(0))
            attention_weights(e + 1, slot(1))
            block_cumsum(slot(2))
            softplus(slot(3))
            logits(e + 4, slot(4))
        return 0

    lax.fori_loop(0, n_steps // PIPE_SLOTS, body, 0)


def _attn_prompt(q, k, v, sb_bias):
    b, t, _ = q.shape
    tq = min(TQ, t)
    kb = min(KB, t)
    n_pairs = ATT_DIM // HEAD_PAIR
    tab, n_steps = _attn_schedule(t, tq, kb)
    seq_spec = pl.BlockSpec((1, t, HEAD_PAIR), lambda bi, p, tab: (bi, 0, p))
    grid_spec = pltpu.PrefetchScalarGridSpec(
        num_scalar_prefetch=1,
        grid=(b, n_pairs),
        in_specs=[pl.BlockSpec(memory_space=pltpu.SMEM), seq_spec, seq_spec, seq_spec,
                  pl.BlockSpec((kb, kb), lambda bi, p, tab: (0, 0))],
        out_specs=seq_spec,
        scratch_shapes=[pltpu.VMEM((PIPE_SLOTS, 2 * tq, kb), F32),
                        pltpu.VMEM((PIPE_SLOTS, 2 * tq, kb), BF16),
                        pltpu.VMEM((PIPE_SLOTS, 2 * tq, kb), F32),
                        pltpu.VMEM((PIPE_SLOTS, 2 * tq, kb), BF16),
                        pltpu.VMEM((1 + kb // tq, 2 * tq, kb), F32),
                        pltpu.VMEM((2 * tq, HEAD_PAIR), F32),
                        pltpu.VMEM((2 * tq, LANES), F32)],
    )
    return pl.pallas_call(
        functools.partial(_attn_prompt_body, n_steps),
        grid_spec=grid_spec,
        out_shape=jax.ShapeDtypeStruct((b, t, ATT_DIM), BF16),
        compiler_params=_params(("arbitrary", "arbitrary")),
        name="attn_prompt",
    )(jnp.asarray(tab), sb_bias, q, k, v, _tri(kb))


def _attn_sample_body(n_steps, pps, pt_ref, bias_ref, q_ref, kn_ref, vn_ref, *refs):
    del pt_ref
    k_refs = refs[:pps]
    v_refs = refs[pps:2 * pps]
    t_ref, o_ref, acc_ref, carry_ref, knew_ref, vnew_ref = refs[2 * pps:]
    j = pl.program_id(1)
    page = t_ref.shape[1]
    rows = N_HEADS * Q_PAD

    def blocks(loaders, masked):
        def side_by_side(which, h):
            return jnp.concatenate([ld[which](h) for ld in loaders], axis=1).astype(BF16)

        z = jnp.concatenate(
            [_dot(q_ref[0, h], side_by_side(0, h)) + bias_ref[h] for h in range(N_HEADS)], axis=0)
        if masked:
            r = lax.broadcasted_iota(jnp.int32, z.shape, 0)
            col = lax.broadcasted_iota(jnp.int32, z.shape, 1)
            z = jnp.where(jnp.logical_or(col >= page, col < lax.rem(r, Q_PAD)), z, MASKED_LOGIT)
        sp = _softplus(z)
        own = z - sp
        sp = sp.astype(BF16)
        carry = carry_ref[...]
        later = []
        for s in range(len(loaders)):
            d = _dot(sp[:, s * page:(s + 1) * page], t_ref[...])
            later.append(d + carry)
            total = d[:, 0:1] + sp[:, s * page:s * page + 1].astype(F32)
            carry = carry + jnp.broadcast_to(total, carry.shape)
        carry_ref[...] = carry
        ab = jnp.exp(own - jnp.concatenate(later, axis=1)).astype(BF16)
        for h in range(N_HEADS):
            acc_ref[h] += _dot_nt(ab[h * Q_PAD:(h + 1) * Q_PAD], side_by_side(1, h))

    pages = [(lambda h, kr=kr: kr[h], lambda h, vr=vr: vr[h]) for kr, vr in zip(k_refs, v_refs)]

    @pl.when(j == 0)
    def _():
        acc_ref[...] = jnp.zeros_like(acc_ref)
        carry_ref[...] = jnp.zeros_like(carry_ref)
        knew_ref[...] = jnp.zeros_like(knew_ref)
        vnew_ref[...] = jnp.zeros_like(vnew_ref)
        n_new = kn_ref.shape[3]
        knew_ref[:, :, 0:n_new] = kn_ref[0]
        vnew_ref[:, :, 0:n_new] = vn_ref[0]
        blocks([(lambda h: knew_ref[h], lambda h: vnew_ref[h])] + pages, True)

    @pl.when(j > 0)
    def _():
        blocks(pages, False)

    @pl.when(j == n_steps - 1)
    def _():
        o_ref[0] = acc_ref[...]


def _attn_sample(q, k_new, v_new, cache_k, cache_v, page_table, sb_bias):
    n_seq = q.shape[0]
    page = cache_k.shape[4]
    n_pages = page_table.shape[1]
    pps = min(PAGES_PER_STEP, n_pages)
    n_steps = n_pages // pps

    def page_spec(s):
        return pl.BlockSpec((None, None, N_HEADS, HEAD_DIM, page),
                            lambda b, j, pt: (0, pt[b, n_pages - 1 - (j * pps + s)], 0, 0, 0))

    seq4 = lambda r, c: pl.BlockSpec((1, N_HEADS, r, c), lambda b, j, pt: (b, 0, 0, 0))
    n_new = k_new.shape[3]
    grid_spec = pltpu.PrefetchScalarGridSpec(
        num_scalar_prefetch=1,
        grid=(n_seq, n_steps),
        in_specs=[pl.BlockSpec(memory_space=pltpu.SMEM), seq4(Q_PAD, HEAD_DIM),
                  seq4(HEAD_DIM, n_new), seq4(HEAD_DIM, n_new)]
                 + [page_spec(s) for s in range(pps)] * 2
                 + [pl.BlockSpec((page, page), lambda b, j, pt: (0, 0))],
        out_specs=seq4(Q_PAD, HEAD_DIM),
        scratch_shapes=[pltpu.VMEM((N_HEADS, Q_PAD, HEAD_DIM), F32),
                        pltpu.VMEM((N_HEADS * Q_PAD, LANES), F32),
                        pltpu.VMEM((N_HEADS, HEAD_DIM, page), F32),
                        pltpu.VMEM((N_HEADS, HEAD_DIM, page), F32)],
    )
    return pl.pallas_call(
        functools.partial(_attn_sample_body, n_steps, pps),
        grid_spec=grid_spec,
        out_shape=jax.ShapeDtypeStruct((n_seq, N_HEADS, Q_PAD, HEAD_DIM), F32),
        compiler_params=_params(("arbitrary", "arbitrary")),
        name="attn_sample",
    )(page_table, sb_bias, q, k_new, v_new, *([cache_k] * pps), *([cache_v] * pps), _tri(page))


def _merge_body(x_ref, att_ref, ga_ref, cb_ref, wba_ref, wout_ref, g2_ref, wrh_ref, wrl_ref, rb_ref,
                h_ref, xn_ref, gate_ref):
    merged = ga_ref[...].astype(F32) * _dot(att_ref[...], wba_ref[...]) + cb_ref[...].astype(F32)
    h = x_ref[...] + _dot(merged.astype(BF16), wout_ref[...])
    h_ref[...] = h
    xn = _rms(h, g2_ref[...])
    xh, xl = _split_bf16(xn)
    xn_ref[...] = xh
    wrh = wrh_ref[...]
    logits = _dot(xh, wrh) + _dot(xl, wrh) + _dot(xh, wrl_ref[...])
    scores = _sigmoid(logits)
    work = scores + rb_ref[...]
    n_exp = work.shape[1]
    lane = lax.broadcasted_iota(jnp.int32, work.shape, 1)
    chosen = jnp.zeros(work.shape, jnp.bool_)
    for _ in range(TOP_K):
        m = jnp.max(work, axis=-1, keepdims=True)
        first = jnp.min(jnp.where(work == m, lane, n_exp), axis=-1, keepdims=True)
        sel = lane == first
        chosen = jnp.logical_or(chosen, sel)
        work = jnp.where(sel, -jnp.inf, work)
    picked = jnp.where(chosen, scores, 0.0)
    gate_ref[...] = picked / jnp.sum(picked, axis=-1, keepdims=True) * ROUTED_SCALE


def _merge(x, att, ga, cb, wba_bf, wout_bf, g2, wr_hi, wr_lo, rbias):
    n, d = x.shape
    tm = min(TM_PROJ, n)
    n_exp = wr_hi.shape[1]
    tok = lambda w: pl.BlockSpec((tm, w), lambda i: (i, 0))
    full = lambda shp: pl.BlockSpec(shp, lambda i: (0,) * len(shp))
    return pl.pallas_call(
        _merge_body,
        grid=(n // tm,),
        in_specs=[tok(d), tok(ATT_DIM), tok(d), tok(d), full((ATT_DIM, d)), full((d, d)), full((1, d)),
                  full((d, n_exp)), full((d, n_exp)), full((1, n_exp))],
        out_specs=(tok(d), tok(d), tok(n_exp)),
        out_shape=(jax.ShapeDtypeStruct((n, d), F32), jax.ShapeDtypeStruct((n, d), BF16),
                   jax.ShapeDtypeStruct((n, n_exp), F32)),
        compiler_params=_params(("arbitrary",)),
        name="merge_router",
    )(x, att, ga, cb, wba_bf, wout_bf, g2, wr_hi, wr_lo, rbias)


def _swiglu_hidden(x, wg, wu):
    a = _dot(x, wg)
    return (a * _sigmoid(a)) * _dot(x, wu)


def _moe_body(xn_ref, h_ref, gate_ref, wg_ref, wu_ref, wd_ref, wgs_ref, wus_ref, wds_ref, gf_ref, y_ref):
    s = pl.program_id(1)
    x = xn_ref[...]

    @pl.when(s == 0)
    def _():
        y_ref[...] = _dot(_swiglu_hidden(x, wgs_ref[...], wus_ref[...]).astype(BF16), wds_ref[...])

    gate = gate_ref[...]
    lane = lax.broadcasted_iota(jnp.int32, gate.shape, 1)
    hidden = []
    for j in range(wg_ref.shape[0]):
        g_e = jnp.sum(jnp.where(lane == s * wg_ref.shape[0] + j, gate, 0.0), axis=-1, keepdims=True)
        hidden.append((g_e * _swiglu_hidden(x, wg_ref[j], wu_ref[j])).astype(BF16))
    y_ref[...] += _dot(jnp.concatenate(hidden, axis=1), wd_ref[...])

    @pl.when(s == pl.num_programs(1) - 1)
    def _():
        y_ref[...] = _rms(h_ref[...] + y_ref[...], gf_ref[...])


def _token_tile(n, largest):
    for tm in range(largest - largest % 16, 0, -16):
        if n % tm == 0:
            return tm
    return n


def _moe(xn, h, gate, wg_bf, wu_bf, wd_bf, wgs_bf, wus_bf, wds_bf, gf):
    n, d = h.shape
    tm = _token_tile(n, TM_MOE)
    n_exp, _, de = wg_bf.shape
    per = EXPERTS_PER_STEP
    assert n_exp % per == 0
    ds = wgs_bf.shape[1]
    once = pl.Buffered(1)
    tok = lambda w: pl.BlockSpec((tm, w), lambda i, s: (i, 0), pipeline_mode=once)
    full = lambda shp: pl.BlockSpec(shp, lambda i, s: (0,) * len(shp), pipeline_mode=once)
    return pl.pallas_call(
        _moe_body,
        grid=(n // tm, n_exp // per),
        in_specs=[tok(d), tok(d), tok(n_exp),
                  pl.BlockSpec((per, d, de), lambda i, s: (s, 0, 0)),
                  pl.BlockSpec((per, d, de), lambda i, s: (s, 0, 0)),
                  pl.BlockSpec((per * de, d), lambda i, s: (s, 0)),
                  full((d, ds)), full((d, ds)), full((ds, d)), full((1, d))],
        out_specs=pl.BlockSpec((tm, d), lambda i, s: (i, 0)),
        out_shape=jax.ShapeDtypeStruct((n, d), F32),
        compiler_params=_params(("arbitrary", "arbitrary")),
        name="moe",
    )(xn, h, gate, wg_bf, wu_bf, wd_bf.reshape(n_exp * de, d), wgs_bf, wus_bf, wds_bf, gf)


def kernel(x_prompt, x_sample, cache_k, cache_v, state_conv, page_table, norm_mix_g, w_in, sb_bias, conv_w,
           w_branch_att, w_branch_conv, w_out, norm_ffn_g, w_router, router_bias, w_gate_e, w_up_e, w_down_e,
           w_gate_s, w_up_s, w_down_s, norm_final_g):
    depth = w_in.shape[0]
    assert depth == 1, "single trunk layer"
    b, t, d = x_prompt.shape
    db, ts, _ = x_sample.shape
    assert ts <= 8 and (CONV_DIM, ATT_DIM) == (state_conv.shape[-1], cache_k.shape[-2] * cache_k.shape[-1])
    l = 0
    row = lambda g: g.reshape(1, -1)
    w_in_bf = w_in[l].astype(BF16)
    wbc_bf = w_branch_conv[l].astype(BF16)
    wba_bf = w_branch_att[l].astype(BF16)
    wout_bf = w_out[l].astype(BF16)
    wr_hi = w_router[l].astype(BF16)
    wr_lo = (w_router[l] - wr_hi.astype(F32)).astype(BF16)
    moe_w = tuple(w[l].astype(BF16) for w in (w_gate_e, w_up_e, w_down_e, w_gate_s, w_up_s, w_down_s))
    bias = sb_bias[l]

    def channel_mixer(x2d, att, ga, cb):
        h, xn, gate = _merge(x2d, att, ga, cb, wba_bf, wout_bf, row(norm_ffn_g[l]), wr_hi, wr_lo,
                             row(router_bias[l]))
        return _moe(xn, h, gate, *moe_w, row(norm_final_g))

    q, k_bf, v_bf, k_f, v_f, ga, cb, conv_tail = _inproj_prompt(
        x_prompt, row(norm_mix_g[l]), w_in_bf, conv_w[l], wbc_bf)
    att = _attn_prompt(q, k_bf, v_bf, bias)
    n = b * t
    y_prompt = channel_mixer(x_prompt.reshape(n, d), att.reshape(n, ATT_DIM), ga.reshape(n, d),
                             cb.reshape(n, d)).reshape(b, t, d)

    ns = db * ts
    st = state_conv[l]
    n_hist = st.shape[1]
    hist1 = jnp.broadcast_to(st[:, n_hist - 1:n_hist], (db, ts, CONV_DIM)).reshape(ns, CONV_DIM)
    hist2 = jnp.concatenate([st, jnp.zeros((db, ts - n_hist, CONV_DIM), F32)], axis=1).reshape(ns, CONV_DIM)
    q_s, k_s, v_s, u_s, ga_s, cb_s = _inproj_sample(
        x_sample.reshape(ns, d), row(norm_mix_g[l]), w_in_bf, conv_w[l], wbc_bf, hist1, hist2, ts)

    by_head = lambda a: a.reshape(db, ts, N_HEADS, HEAD_DIM)
    q_h = jnp.pad(by_head(q_s).transpose(0, 2, 1, 3), ((0, 0), (0, 0), (0, Q_PAD - ts), (0, 0))).astype(BF16)
    dim_major = lambda c: c.transpose(0, 1, 3, 4, 2)
    att_s = _attn_sample(q_h, by_head(k_s).transpose(0, 2, 3, 1), by_head(v_s).transpose(0, 2, 3, 1),
                         dim_major(cache_k), dim_major(cache_v), page_table, bias)
    att_s = att_s[:, :, :ts].transpose(0, 2, 1, 3).reshape(ns, ATT_DIM).astype(BF16)
    y_sample = channel_mixer(x_sample.reshape(ns, d), att_s, ga_s, cb_s).reshape(db, ts, d)

    heads = lambda a, lead: a.reshape(1, *lead, N_HEADS, HEAD_DIM)
    return (y_prompt, y_sample,
            heads(k_f, (b, t)), heads(v_f, (b, t)), conv_tail[:, 6:8].reshape(1, b, 2, CONV_DIM),
            heads(k_s, (db, ts)), heads(v_s, (db, ts)),
            u_s.reshape(db, ts, CONV_DIM)[:, ts - 2:].reshape(1, db, 2, CONV_DIM))
```

```python
import functools

import jax
import jax.numpy as jnp
import numpy as np
from jax import lax
from jax.experimental import pallas as pl
from jax.experimental.pallas import tpu as pltpu

F32 = jnp.float32
BF16 = jnp.bfloat16

N_HEADS = 8
HEAD_DIM = 64
ATT_DIM = N_HEADS * HEAD_DIM
CONV_DIM = 512
TOP_K = 8
ROUTED_SCALE = 2.5
EPS = 1e-6

VMEM_LIMIT_BYTES = 56 * 1024 * 1024
LANES = 128
HEAD_PAIR = LANES

TM_PROJ = 512
TM_MOE = 1024
EXPERTS_PER_STEP = 4
TQ = 128
KB = 256
Q_PAD = 16
PAGES_PER_STEP = 8
PIPE = 4
PIPE_SLOTS = PIPE + 1
MASKED_LOGIT = -1e30
SOFTPLUS_CLAMP = 80.0


def _params(sem):
    return pltpu.CompilerParams(dimension_semantics=sem, vmem_limit_bytes=VMEM_LIMIT_BYTES)


def _rms(x, g):
    return (x * lax.rsqrt(jnp.mean(x * x, axis=-1, keepdims=True) + EPS)) * g


def _sigmoid(x):
    return 1.0 / (1.0 + jnp.exp(-x))


def _softplus(z):
    return jnp.maximum(z, jnp.log(1.0 + jnp.exp(jnp.minimum(z, SOFTPLUS_CLAMP))))


def _dot(a, b):
    return jnp.dot(a, b, preferred_element_type=F32)


def _dot_nt(a, b):
    return lax.dot_general(a, b, (((1,), (1,)), ((), ())), preferred_element_type=F32)


def _split_bf16(x):
    hi = x.astype(BF16)
    lo = (x - hi.astype(F32)).astype(BF16)
    return hi, lo


def _tri(n):
    j = lax.broadcasted_iota(jnp.int32, (n, n), 0)
    s = lax.broadcasted_iota(jnp.int32, (n, n), 1)
    return (j > s).astype(BF16)


def _project(xb, w_ref, lo, hi):
    return _dot(xb, w_ref[:, lo:hi])


def _conv_branch(xb, w_ref, cw_ref, wbc_ref, u, um1, um2):
    a = ATT_DIM
    cw = cw_ref[...]
    conv = _project(xb, w_ref, 3 * a, 3 * a + CONV_DIM) * (cw[0:1] * um2 + cw[1:2] * um1 + cw[2:3] * u)
    d = w_ref.shape[0]
    g_conv = _project(xb, w_ref, 3 * a + 3 * CONV_DIM + d, 3 * a + 3 * CONV_DIM + 2 * d)
    return _sigmoid(g_conv) * _dot(conv.astype(BF16), wbc_ref[...])


def _inproj_prompt_body(x_ref, g_ref, w_ref, cw_ref, wbc_ref,
                        q_ref, kb_ref, vb_ref, kf_ref, vf_ref, ga_ref, cb_ref, cs_ref, hist_ref):
    a, c, d = ATT_DIM, CONV_DIM, x_ref.shape[2]
    tm = x_ref.shape[1]
    xb = _rms(x_ref[0], g_ref[...]).astype(BF16)
    q_ref[0] = (_project(xb, w_ref, 0, a) * (HEAD_DIM ** -0.5)).astype(BF16)
    k = _project(xb, w_ref, a, 2 * a)
    kf_ref[0] = k
    kb_ref[0] = k.astype(BF16)
    v = _project(xb, w_ref, 2 * a, 3 * a)
    vf_ref[0] = v
    vb_ref[0] = v.astype(BF16)
    u = _project(xb, w_ref, 3 * a + c, 3 * a + 2 * c) * _project(xb, w_ref, 3 * a + 2 * c, 3 * a + 3 * c)

    @pl.when(pl.program_id(1) == 0)
    def _():
        hist_ref[...] = jnp.zeros_like(hist_ref)

    row = lax.broadcasted_iota(jnp.int32, u.shape, 0)
    h0 = hist_ref[6:7, :]
    h1 = hist_ref[7:8, :]
    um1 = jnp.where(row < 1, h1, pltpu.roll(u, 1, 0))
    um2 = jnp.where(row < 1, h0, jnp.where(row < 2, h1, pltpu.roll(u, 2, 0)))
    cb_ref[0] = _conv_branch(xb, w_ref, cw_ref, wbc_ref, u, um1, um2).astype(BF16)
    tail = u[tm - 8:tm, :]
    hist_ref[...] = tail
    cs_ref[0] = tail
    ga_ref[0] = _sigmoid(_project(xb, w_ref, 3 * a + 3 * c, 3 * a + 3 * c + d)).astype(BF16)


def _inproj_prompt(x, g, w_bf, conv_w, wbc_bf):
    b, t, d = x.shape
    tm = min(TM_PROJ, t)
    nt = t // tm
    n_in = w_bf.shape[1]
    tok = lambda n: pl.BlockSpec((1, tm, n), lambda i, j: (i, j, 0))
    full = lambda shp: pl.BlockSpec(shp, lambda i, j: (0,) * len(shp))
    out_shape = (
        jax.ShapeDtypeStruct((b, t, ATT_DIM), BF16),
        jax.ShapeDtypeStruct((b, t, ATT_DIM), BF16),
        jax.ShapeDtypeStruct((b, t, ATT_DIM), BF16),
        jax.ShapeDtypeStruct((b, t, ATT_DIM), F32),
        jax.ShapeDtypeStruct((b, t, ATT_DIM), F32),
        jax.ShapeDtypeStruct((b, t, d), BF16),
        jax.ShapeDtypeStruct((b, t, d), BF16),
        jax.ShapeDtypeStruct((b, 8, CONV_DIM), F32),
    )
    return pl.pallas_call(
        _inproj_prompt_body,
        grid=(b, nt),
        in_specs=[tok(d), full((1, d)), full((d, n_in)), full((3, CONV_DIM)), full((CONV_DIM, d))],
        out_specs=(tok(ATT_DIM), tok(ATT_DIM), tok(ATT_DIM), tok(ATT_DIM), tok(ATT_DIM), tok(d), tok(d),
                   pl.BlockSpec((1, 8, CONV_DIM), lambda i, j: (i, 0, 0))),
        out_shape=out_shape,
        scratch_shapes=[pltpu.VMEM((8, CONV_DIM), F32)],
        compiler_params=_params(("arbitrary", "arbitrary")),
        name="inproj_prompt",
    )(x, g, w_bf, conv_w, wbc_bf)


def _inproj_sample_body(ts, x_ref, g_ref, w_ref, cw_ref, wbc_ref, h1_ref, h2_ref,
                        q_ref, k_ref, v_ref, u_ref, ga_ref, cb_ref):
    a, c, d = ATT_DIM, CONV_DIM, x_ref.shape[1]
    xb = _rms(x_ref[...], g_ref[...]).astype(BF16)
    q_ref[...] = _project(xb, w_ref, 0, a) * (HEAD_DIM ** -0.5)
    k_ref[...] = _project(xb, w_ref, a, 2 * a)
    v_ref[...] = _project(xb, w_ref, 2 * a, 3 * a)
    u = _project(xb, w_ref, 3 * a + c, 3 * a + 2 * c) * _project(xb, w_ref, 3 * a + 2 * c, 3 * a + 3 * c)
    u_ref[...] = u
    tok = lax.rem(lax.broadcasted_iota(jnp.int32, u.shape, 0), ts)
    um1 = jnp.where(tok >= 1, pltpu.roll(u, 1, 0), h1_ref[...])
    um2 = jnp.where(tok >= 2, pltpu.roll(u, 2, 0), h2_ref[...])
    cb_ref[...] = _conv_branch(xb, w_ref, cw_ref, wbc_ref, u, um1, um2).astype(BF16)
    ga_ref[...] = _sigmoid(_project(xb, w_ref, 3 * a + 3 * c, 3 * a + 3 * c + d)).astype(BF16)


def _inproj_sample(x, g, w_bf, conv_w, wbc_bf, hist1, hist2, ts):
    n, d = x.shape
    out_shape = (
        jax.ShapeDtypeStruct((n, ATT_DIM), F32),
        jax.ShapeDtypeStruct((n, ATT_DIM), F32),
        jax.ShapeDtypeStruct((n, ATT_DIM), F32),
        jax.ShapeDtypeStruct((n, CONV_DIM), F32),
        jax.ShapeDtypeStruct((n, d), BF16),
        jax.ShapeDtypeStruct((n, d), BF16),
    )
    return pl.pallas_call(
        functools.partial(_inproj_sample_body, ts),
        out_shape=out_shape,
        compiler_params=pltpu.CompilerParams(vmem_limit_bytes=VMEM_LIMIT_BYTES),
        name="inproj_sample",
    )(x, g, w_bf, conv_w, wbc_bf, hist1, hist2)


def _attn_schedule(t, tq, kb):
    items = []
    for i in range(t // tq):
        jd = (i * tq) // kb
        for n in range(jd + 1):
            diag = 1 + ((i * tq) % kb) // tq if n == 0 else 0
            items.append((i * tq, (jd - n) * kb, diag, int(n == 0), int(n == jd)))
    idle = (0, 0, 0, 1, 0)
    n_steps = -(-(len(items) + PIPE) // PIPE_SLOTS) * PIPE_SLOTS
    assert items[0][3] == 1 and items[0][4] == 1
    items = [idle] * PIPE + items + [items[0]] * (n_steps - len(items))
    return np.asarray(items, np.int32).T, n_steps


def _attn_prompt_body(n_steps, tab_ref, bias_ref, q_ref, k_ref, v_ref, t_ref, o_ref,
                      z_ref, sp_ref, d_ref, a_ref, bm_ref, acc_ref, carry_ref):
    p = pl.program_id(1)
    rows, kb = bm_ref.shape[1], bm_ref.shape[2]
    tq = rows // 2
    n_diag = bm_ref.shape[0] - 1

    r = lax.broadcasted_iota(jnp.int32, (rows, kb), 0)
    col = lax.broadcasted_iota(jnp.int32, (rows, kb), 1)
    bias = jnp.where(r < tq, bias_ref[2 * p], bias_ref[2 * p + 1])
    bm_ref[0] = bias
    for dgn in range(n_diag):
        bm_ref[1 + dgn] = jnp.where(col < dgn * tq + lax.rem(r, tq), bias, MASKED_LOGIT)
    z_ref[...] = jnp.full(z_ref.shape, MASKED_LOGIT, F32)
    sp_ref[...] = jnp.zeros_like(sp_ref)
    d_ref[...] = jnp.zeros_like(d_ref)
    a_ref[...] = jnp.zeros_like(a_ref)
    acc_ref[...] = jnp.zeros_like(acc_ref)
    carry_ref[...] = jnp.zeros_like(carry_ref)
    lane = lax.broadcasted_iota(jnp.int32, (tq, HEAD_PAIR), 1)

    def logits(e, slot):
        q = q_ref[0, pl.ds(pl.multiple_of(tab_ref[0, e], tq), tq), :]
        zero = jnp.zeros_like(q)
        q2 = jnp.concatenate([jnp.where(lane < HEAD_DIM, q, zero), jnp.where(lane >= HEAD_DIM, q, zero)], axis=0)
        kblk = k_ref[0, pl.ds(pl.multiple_of(tab_ref[1, e], kb), kb), :]
        z_ref[slot] = _dot_nt(q2, kblk) + bm_ref[tab_ref[2, e]]

    def softplus(slot):
        z = z_ref[slot]
        sp = _softplus(z)
        z_ref[slot] = z - sp
        sp_ref[slot] = sp.astype(BF16)

    def block_cumsum(slot):
        d_ref[slot] = _dot(sp_ref[slot], t_ref[...])

    def attention_weights(e, slot):
        carry = jnp.where(tab_ref[3, e] == 1, 0.0, carry_ref[...])
        d = d_ref[slot]
        c = d + jnp.concatenate([carry] * (kb // LANES), axis=1)
        a_ref[slot] = jnp.exp(z_ref[slot] - c).astype(BF16)
        total = d[:, 0:1] + sp_ref[slot, :, 0:1].astype(F32)
        carry_ref[...] = carry + jnp.broadcast_to(total, carry_ref.shape)

    def weights_times_values(e, slot):
        vblk = v_ref[0, pl.ds(pl.multiple_of(tab_ref[1, e], kb), kb), :]
        acc = jnp.where(tab_ref[3, e] == 1, 0.0, acc_ref[...]) + _dot(a_ref[slot], vblk)
        acc_ref[...] = acc
        o_ref[0, pl.ds(pl.multiple_of(tab_ref[0, e], tq), tq), :] = jnp.where(
            lane < HEAD_DIM, acc[:tq], acc[tq:]).astype(o_ref.dtype)

    def body(it, _):
        for u in range(PIPE_SLOTS):
            e = it * PIPE_SLOTS + u
            slot = lambda s, u=u: (u + s) % PIPE_SLOTS
            weights_times_values(e, slot(0))
            attention_weights(e + 1, slot(1))
            block_cumsum(slot(2))
            softplus(slot(3))
            logits(e + 4, slot(4))
        return 0

    lax.fori_loop(0, n_steps // PIPE_SLOTS, body, 0)


def _attn_prompt(q, k, v, sb_bias):
    b, t, _ = q.shape
    tq = min(TQ, t)
    kb = min(KB, t)
    n_pairs = ATT_DIM // HEAD_PAIR
    tab, n_steps = _attn_schedule(t, tq, kb)
    seq_spec = pl.BlockSpec((1, t, HEAD_PAIR), lambda bi, p, tab: (bi, 0, p))
    grid_spec = pltpu.PrefetchScalarGridSpec(
        num_scalar_prefetch=1,
        grid=(b, n_pairs),
        in_specs=[pl.BlockSpec(memory_space=pltpu.SMEM), seq_spec, seq_spec, seq_spec,
                  pl.BlockSpec((kb, kb), lambda bi, p, tab: (0, 0))],
        out_specs=seq_spec,
        scratch_shapes=[pltpu.VMEM((PIPE_SLOTS, 2 * tq, kb), F32),
                        pltpu.VMEM((PIPE_SLOTS, 2 * tq, kb), BF16),
                        pltpu.VMEM((PIPE_SLOTS, 2 * tq, kb), F32),
                        pltpu.VMEM((PIPE_SLOTS, 2 * tq, kb), BF16),
                        pltpu.VMEM((1 + kb // tq, 2 * tq, kb), F32),
                        pltpu.VMEM((2 * tq, HEAD_PAIR), F32),
                        pltpu.VMEM((2 * tq, LANES), F32)],
    )
    return pl.pallas_call(
        functools.partial(_attn_prompt_body, n_steps),
        grid_spec=grid_spec,
        out_shape=jax.ShapeDtypeStruct((b, t, ATT_DIM), BF16),
        compiler_params=_params(("arbitrary", "arbitrary")),
        name="attn_prompt",
    )(jnp.asarray(tab), sb_bias, q, k, v, _tri(kb))


def _attn_sample_body(n_steps, pps, pt_ref, bias_ref, q_ref, kn_ref, vn_ref, *refs):
    del pt_ref
    k_refs = refs[:pps]
    v_refs = refs[pps:2 * pps]
    t_ref, o_ref, acc_ref, carry_ref, knew_ref, vnew_ref = refs[2 * pps:]
    j = pl.program_id(1)
    page = t_ref.shape[1]
    rows = N_HEADS * Q_PAD

    def blocks(loaders, masked):
        def side_by_side(which, h):
            return jnp.concatenate([ld[which](h) for ld in loaders], axis=1).astype(BF16)

        z = jnp.concatenate(
            [_dot(q_ref[0, h], side_by_side(0, h)) + bias_ref[h] for h in range(N_HEADS)], axis=0)
        if masked:
            r = lax.broadcasted_iota(jnp.int32, z.shape, 0)
            col = lax.broadcasted_iota(jnp.int32, z.shape, 1)
            z = jnp.where(jnp.logical_or(col >= page, col < lax.rem(r, Q_PAD)), z, MASKED_LOGIT)
        sp = _softplus(z)
        own = z - sp
        sp = sp.astype(BF16)
        carry = carry_ref[...]
        later = []
        for s in range(len(loaders)):
            d = _dot(sp[:, s * page:(s + 1) * page], t_ref[...])
            later.append(d + carry)
            total = d[:, 0:1] + sp[:, s * page:s * page + 1].astype(F32)
            carry = carry + jnp.broadcast_to(total, carry.shape)
        carry_ref[...] = carry
        ab = jnp.exp(own - jnp.concatenate(later, axis=1)).astype(BF16)
        for h in range(N_HEADS):
            acc_ref[h] += _dot_nt(ab[h * Q_PAD:(h + 1) * Q_PAD], side_by_side(1, h))

    pages = [(lambda h, kr=kr: kr[h], lambda h, vr=vr: vr[h]) for kr, vr in zip(k_refs, v_refs)]

    @pl.when(j == 0)
    def _():
        acc_ref[...] = jnp.zeros_like(acc_ref)
        carry_ref[...] = jnp.zeros_like(carry_ref)
        knew_ref[...] = jnp.zeros_like(knew_ref)
        vnew_ref[...] = jnp.zeros_like(vnew_ref)
        n_new = kn_ref.shape[3]
        knew_ref[:, :, 0:n_new] = kn_ref[0]
        vnew_ref[:, :, 0:n_new] = vn_ref[0]
        blocks([(lambda h: knew_ref[h], lambda h: vnew_ref[h])] + pages, True)

    @pl.when(j > 0)
    def _():
        blocks(pages, False)

    @pl.when(j == n_steps - 1)
    def _():
        o_ref[0] = acc_ref[...]


def _attn_sample(q, k_new, v_new, cache_k, cache_v, page_table, sb_bias):
    n_seq = q.shape[0]
    page = cache_k.shape[4]
    n_pages = page_table.shape[1]
    pps = min(PAGES_PER_STEP, n_pages)
    n_steps = n_pages // pps

    def page_spec(s):
        return pl.BlockSpec((None, None, N_HEADS, HEAD_DIM, page),
                            lambda b, j, pt: (0, pt[b, n_pages - 1 - (j * pps + s)], 0, 0, 0))

    seq4 = lambda r, c: pl.BlockSpec((1, N_HEADS, r, c), lambda b, j, pt: (b, 0, 0, 0))
    n_new = k_new.shape[3]
    grid_spec = pltpu.PrefetchScalarGridSpec(
        num_scalar_prefetch=1,
        grid=(n_seq, n_steps),
        in_specs=[pl.BlockSpec(memory_space=pltpu.SMEM), seq4(Q_PAD, HEAD_DIM),
                  seq4(HEAD_DIM, n_new), seq4(HEAD_DIM, n_new)]
                 + [page_spec(s) for s in range(pps)] * 2
                 + [pl.BlockSpec((page, page), lambda b, j, pt: (0, 0))],
        out_specs=seq4(Q_PAD, HEAD_DIM),
        scratch_shapes=[pltpu.VMEM((N_HEADS, Q_PAD, HEAD_DIM), F32),
                        pltpu.VMEM((N_HEADS * Q_PAD, LANES), F32),
                        pltpu.VMEM((N_HEADS, HEAD_DIM, page), F32),
                        pltpu.VMEM((N_HEADS, HEAD_DIM, page), F32)],
    )
    return pl.pallas_call(
        functools.partial(_attn_sample_body, n_steps, pps),
        grid_spec=grid_spec,
        out_shape=jax.ShapeDtypeStruct((n_seq, N_HEADS, Q_PAD, HEAD_DIM), F32),
        compiler_params=_params(("arbitrary", "arbitrary")),
        name="attn_sample",
    )(page_table, sb_bias, q, k_new, v_new, *([cache_k] * pps), *([cache_v] * pps), _tri(page))


def _merge_body(x_ref, att_ref, ga_ref, cb_ref, wba_ref, wout_ref, g2_ref, wrh_ref, wrl_ref, rb_ref,
                h_ref, xn_ref, gate_ref):
    merged = ga_ref[...].astype(F32) * _dot(att_ref[...], wba_ref[...]) + cb_ref[...].astype(F32)
    h = x_ref[...] + _dot(merged.astype(BF16), wout_ref[...])
    h_ref[...] = h
    xn = _rms(h, g2_ref[...])
    xh, xl = _split_bf16(xn)
    xn_ref[...] = xh
    wrh = wrh_ref[...]
    logits = _dot_nt(wrh, xh) + _dot_nt(wrh, xl) + _dot_nt(wrl_ref[...], xh)
    scores = _sigmoid(logits)
    work = scores + rb_ref[...]
    n_exp = work.shape[0]
    expert = lax.broadcasted_iota(jnp.int32, work.shape, 0)
    chosen = jnp.zeros(work.shape, jnp.bool_)
    for _ in range(TOP_K):
        m = jnp.max(work, axis=0, keepdims=True)
        first = jnp.min(jnp.where(work == m, expert, n_exp), axis=0, keepdims=True)
        sel = expert == first
        chosen = jnp.logical_or(chosen, sel)
        work = jnp.where(sel, -jnp.inf, work)
    picked = jnp.where(chosen, scores, 0.0)
    gate = picked / jnp.sum(picked, axis=0, keepdims=True) * ROUTED_SCALE
    gate_ref[...] = gate.T


def _merge(x, att, ga, cb, wba_bf, wout_bf, g2, wr_hi, wr_lo, rbias):
    n, d = x.shape
    tm = min(TM_PROJ, n)
    n_exp = wr_hi.shape[0]
    tok = lambda w: pl.BlockSpec((tm, w), lambda i: (i, 0))
    full = lambda shp: pl.BlockSpec(shp, lambda i: (0,) * len(shp))
    return pl.pallas_call(
        _merge_body,
        grid=(n // tm,),
        in_specs=[tok(d), tok(ATT_DIM), tok(d), tok(d), full((ATT_DIM, d)), full((d, d)), full((1, d)),
                  full((n_exp, d)), full((n_exp, d)), full((n_exp, 1))],
        out_specs=(tok(d), tok(d), tok(n_exp)),
        out_shape=(jax.ShapeDtypeStruct((n, d), F32), jax.ShapeDtypeStruct((n, d), BF16),
                   jax.ShapeDtypeStruct((n, n_exp), F32)),
        compiler_params=_params(("arbitrary",)),
        name="merge_router",
    )(x, att, ga, cb, wba_bf, wout_bf, g2, wr_hi, wr_lo, rbias)


def _swiglu_hidden(x, wg, wu):
    a = _dot(x, wg)
    return (a * _sigmoid(a)) * _dot(x, wu)


def _moe_body(xn_ref, h_ref, gate_ref, wg_ref, wu_ref, wd_ref, wgs_ref, wus_ref, wds_ref, gf_ref, y_ref):
    s = pl.program_id(1)
    x = xn_ref[...]

    @pl.when(s == 0)
    def _():
        y_ref[...] = _dot(_swiglu_hidden(x, wgs_ref[...], wus_ref[...]).astype(BF16), wds_ref[...])

    gate = gate_ref[...]
    lane = lax.broadcasted_iota(jnp.int32, gate.shape, 1)
    hidden = []
    for j in range(wg_ref.shape[0]):
        g_e = jnp.sum(jnp.where(lane == s * wg_ref.shape[0] + j, gate, 0.0), axis=-1, keepdims=True)
        hidden.append((g_e * _swiglu_hidden(x, wg_ref[j], wu_ref[j])).astype(BF16))
    y_ref[...] += _dot(jnp.concatenate(hidden, axis=1), wd_ref[...])

    @pl.when(s == pl.num_programs(1) - 1)
    def _():
        y_ref[...] = _rms(h_ref[...] + y_ref[...], gf_ref[...])


def _token_tile(n, largest):
    for tm in range(largest - largest % 16, 0, -16):
        if n % tm == 0:
            return tm
    return n


def _moe(xn, h, gate, wg_bf, wu_bf, wd_bf, wgs_bf, wus_bf, wds_bf, gf):
    n, d = h.shape
    tm = _token_tile(n, TM_MOE)
    n_exp, _, de = wg_bf.shape
    per = EXPERTS_PER_STEP
    assert n_exp % per == 0
    ds = wgs_bf.shape[1]
    tok = lambda w: pl.BlockSpec((tm, w), lambda i, s: (i, 0))
    full = lambda shp: pl.BlockSpec(shp, lambda i, s: (0,) * len(shp))
    return pl.pallas_call(
        _moe_body,
        grid=(n // tm, n_exp // per),
        in_specs=[tok(d), tok(d), tok(n_exp),
                  pl.BlockSpec((per, d, de), lambda i, s: (s, 0, 0)),
                  pl.BlockSpec((per, d, de), lambda i, s: (s, 0, 0)),
                  pl.BlockSpec((per * de, d), lambda i, s: (s, 0)),
                  full((d, ds)), full((d, ds)), full((ds, d)), full((1, d))],
        out_specs=tok(d),
        out_shape=jax.ShapeDtypeStruct((n, d), F32),
        compiler_params=_params(("arbitrary", "arbitrary")),
        name="moe",
    )(xn, h, gate, wg_bf, wu_bf, wd_bf.reshape(n_exp * de, d), wgs_bf, wus_bf, wds_bf, gf)


def kernel(x_prompt, x_sample, cache_k, cache_v, state_conv, page_table, norm_mix_g, w_in, sb_bias, conv_w,
           w_branch_att, w_branch_conv, w_out, norm_ffn_g, w_router, router_bias, w_gate_e, w_up_e, w_down_e,
           w_gate_s, w_up_s, w_down_s, norm_final_g):
    depth = w_in.shape[0]
    assert depth == 1, "single trunk layer"
    b, t, d = x_prompt.shape
    db, ts, _ = x_sample.shape
    assert ts <= 8 and (CONV_DIM, ATT_DIM) == (state_conv.shape[-1], cache_k.shape[-2] * cache_k.shape[-1])
    l = 0
    row = lambda g: g.reshape(1, -1)
    w_in_bf = w_in[l].astype(BF16)
    wbc_bf = w_branch_conv[l].astype(BF16)
    wba_bf = w_branch_att[l].astype(BF16)
    wout_bf = w_out[l].astype(BF16)
    wr_t = w_router[l].T
    wr_hi = wr_t.astype(BF16)
    wr_lo = (wr_t - wr_hi.astype(F32)).astype(BF16)
    moe_w = tuple(w[l].astype(BF16) for w in (w_gate_e, w_up_e, w_down_e, w_gate_s, w_up_s, w_down_s))
    bias = sb_bias[l]

    def channel_mixer(x2d, att, ga, cb):
        h, xn, gate = _merge(x2d, att, ga, cb, wba_bf, wout_bf, row(norm_ffn_g[l]), wr_hi, wr_lo,
                             router_bias[l].reshape(-1, 1))
        return _moe(xn, h, gate, *moe_w, row(norm_final_g))

    q, k_bf, v_bf, k_f, v_f, ga, cb, conv_tail = _inproj_prompt(
        x_prompt, row(norm_mix_g[l]), w_in_bf, conv_w[l], wbc_bf)
    att = _attn_prompt(q, k_bf, v_bf, bias)
    n = b * t
    y_prompt = channel_mixer(x_prompt.reshape(n, d), att.reshape(n, ATT_DIM), ga.reshape(n, d),
                             cb.reshape(n, d)).reshape(b, t, d)

    ns = db * ts
    st = state_conv[l]
    n_hist = st.shape[1]
    hist1 = jnp.broadcast_to(st[:, n_hist - 1:n_hist], (db, ts, CONV_DIM)).reshape(ns, CONV_DIM)
    hist2 = jnp.concatenate([st, jnp.zeros((db, ts - n_hist, CONV_DIM), F32)], axis=1).reshape(ns, CONV_DIM)
    q_s, k_s, v_s, u_s, ga_s, cb_s = _inproj_sample(
        x_sample.reshape(ns, d), row(norm_mix_g[l]), w_in_bf, conv_w[l], wbc_bf, hist1, hist2, ts)

    by_head = lambda a: a.reshape(db, ts, N_HEADS, HEAD_DIM)
    q_h = jnp.pad(by_head(q_s).transpose(0, 2, 1, 3), ((0, 0), (0, 0), (0, Q_PAD - ts), (0, 0))).astype(BF16)
    dim_major = lambda c: c.transpose(0, 1, 3, 4, 2)
    att_s = _attn_sample(q_h, by_head(k_s).transpose(0, 2, 3, 1), by_head(v_s).transpose(0, 2, 3, 1),
                         dim_major(cache_k), dim_major(cache_v), page_table, bias)
    att_s = att_s[:, :, :ts].transpose(0, 2, 1, 3).reshape(ns, ATT_DIM).astype(BF16)
    y_sample = channel_mixer(x_sample.reshape(ns, d), att_s, ga_s, cb_s).reshape(db, ts, d)

    heads = lambda a, lead: a.reshape(1, *lead, N_HEADS, HEAD_DIM)
    return (y_prompt, y_sample,
            heads(k_f, (b, t)), heads(v_f, (b, t)), conv_tail[:, 6:8].reshape(1, b, 2, CONV_DIM),
            heads(k_s, (db, ts)), heads(v_s, (db, ts)),
            u_s.reshape(db, ts, CONV_DIM)[:, ts - 2:].reshape(1, db, 2, CONV_DIM))
```

```python
import functools

import jax
import jax.numpy as jnp
import numpy as np
from jax import lax
from jax.experimental import pallas as pl
from jax.experimental.pallas import tpu as pltpu

F32 = jnp.float32
BF16 = jnp.bfloat16

N_HEADS = 8
HEAD_DIM = 64
ATT_DIM = N_HEADS * HEAD_DIM
CONV_DIM = 512
TOP_K = 8
ROUTED_SCALE = 2.5
EPS = 1e-6

VMEM_LIMIT_BYTES = 56 * 1024 * 1024
LANES = 128
HEAD_PAIR = LANES

TM_PROJ = 512
TM_MOE = 1024
EXPERTS_PER_STEP = 4
TQ = 128
KB = 256
Q_PAD = 16
PAGES_PER_STEP = 16
PIPE = 3
PIPE_SLOTS = PIPE + 1
MASKED_LOGIT = -1e30
SOFTPLUS_CLAMP = 80.0


def _params(sem):
    return pltpu.CompilerParams(dimension_semantics=sem, vmem_limit_bytes=VMEM_LIMIT_BYTES)


def _rms(x, g):
    return (x * lax.rsqrt(jnp.mean(x * x, axis=-1, keepdims=True) + EPS)) * g


def _sigmoid(x):
    return 1.0 / (1.0 + jnp.exp(-x))


def _softplus(z):
    return jnp.maximum(z, jnp.log(1.0 + jnp.exp(jnp.minimum(z, SOFTPLUS_CLAMP))))


def _dot(a, b):
    return jnp.dot(a, b, preferred_element_type=F32)


def _dot_nt(a, b):
    return lax.dot_general(a, b, (((1,), (1,)), ((), ())), preferred_element_type=F32)


def _split_bf16(x):
    hi = x.astype(BF16)
    lo = (x - hi.astype(F32)).astype(BF16)
    return hi, lo


def _tri(n):
    j = lax.broadcasted_iota(jnp.int32, (n, n), 0)
    s = lax.broadcasted_iota(jnp.int32, (n, n), 1)
    return (j > s).astype(BF16)


def _project(xb, w_ref, lo, hi):
    return _dot(xb, w_ref[:, lo:hi])


def _conv_branch(xb, w_ref, cw_ref, wbc_ref, u, um1, um2):
    a = ATT_DIM
    cw = cw_ref[...]
    conv = _project(xb, w_ref, 3 * a, 3 * a + CONV_DIM) * (cw[0:1] * um2 + cw[1:2] * um1 + cw[2:3] * u)
    d = w_ref.shape[0]
    g_conv = _project(xb, w_ref, 3 * a + 3 * CONV_DIM + d, 3 * a + 3 * CONV_DIM + 2 * d)
    return _sigmoid(g_conv) * _dot(conv.astype(BF16), wbc_ref[...])


def _inproj_prompt_body(x_ref, g_ref, w_ref, cw_ref, wbc_ref,
                        q_ref, kb_ref, vb_ref, kf_ref, vf_ref, ga_ref, cb_ref, cs_ref, hist_ref):
    a, c, d = ATT_DIM, CONV_DIM, x_ref.shape[2]
    tm = x_ref.shape[1]
    xb = _rms(x_ref[0], g_ref[...]).astype(BF16)
    q_ref[0] = (_project(xb, w_ref, 0, a) * (HEAD_DIM ** -0.5)).astype(BF16)
    k = _project(xb, w_ref, a, 2 * a)
    kf_ref[0] = k
    kb_ref[0] = k.astype(BF16)
    v = _project(xb, w_ref, 2 * a, 3 * a)
    vf_ref[0] = v
    vb_ref[0] = v.astype(BF16)
    u = _project(xb, w_ref, 3 * a + c, 3 * a + 2 * c) * _project(xb, w_ref, 3 * a + 2 * c, 3 * a + 3 * c)

    @pl.when(pl.program_id(1) == 0)
    def _():
        hist_ref[...] = jnp.zeros_like(hist_ref)

    row = lax.broadcasted_iota(jnp.int32, u.shape, 0)
    h0 = hist_ref[6:7, :]
    h1 = hist_ref[7:8, :]
    um1 = jnp.where(row < 1, h1, pltpu.roll(u, 1, 0))
    um2 = jnp.where(row < 1, h0, jnp.where(row < 2, h1, pltpu.roll(u, 2, 0)))
    cb_ref[0] = _conv_branch(xb, w_ref, cw_ref, wbc_ref, u, um1, um2).astype(BF16)
    tail = u[tm - 8:tm, :]
    hist_ref[...] = tail
    cs_ref[0] = tail
    ga_ref[0] = _sigmoid(_project(xb, w_ref, 3 * a + 3 * c, 3 * a + 3 * c + d)).astype(BF16)


def _inproj_prompt(x, g, w_bf, conv_w, wbc_bf):
    b, t, d = x.shape
    tm = min(TM_PROJ, t)
    nt = t // tm
    n_in = w_bf.shape[1]
    tok = lambda n: pl.BlockSpec((1, tm, n), lambda i, j: (i, j, 0))
    full = lambda shp: pl.BlockSpec(shp, lambda i, j: (0,) * len(shp))
    out_shape = (
        jax.ShapeDtypeStruct((b, t, ATT_DIM), BF16),
        jax.ShapeDtypeStruct((b, t, ATT_DIM), BF16),
        jax.ShapeDtypeStruct((b, t, ATT_DIM), BF16),
        jax.ShapeDtypeStruct((b, t, ATT_DIM), F32),
        jax.ShapeDtypeStruct((b, t, ATT_DIM), F32),
        jax.ShapeDtypeStruct((b, t, d), BF16),
        jax.ShapeDtypeStruct((b, t, d), BF16),
        jax.ShapeDtypeStruct((b, 8, CONV_DIM), F32),
    )
    return pl.pallas_call(
        _inproj_prompt_body,
        grid=(b, nt),
        in_specs=[tok(d), full((1, d)), full((d, n_in)), full((3, CONV_DIM)), full((CONV_DIM, d))],
        out_specs=(tok(ATT_DIM), tok(ATT_DIM), tok(ATT_DIM), tok(ATT_DIM), tok(ATT_DIM), tok(d), tok(d),
                   pl.BlockSpec((1, 8, CONV_DIM), lambda i, j: (i, 0, 0))),
        out_shape=out_shape,
        scratch_shapes=[pltpu.VMEM((8, CONV_DIM), F32)],
        compiler_params=_params(("arbitrary", "arbitrary")),
        name="inproj_prompt",
    )(x, g, w_bf, conv_w, wbc_bf)


def _inproj_sample_body(ts, x_ref, g_ref, w_ref, cw_ref, wbc_ref, h1_ref, h2_ref,
                        q_ref, k_ref, v_ref, u_ref, ga_ref, cb_ref):
    a, c, d = ATT_DIM, CONV_DIM, x_ref.shape[1]
    xb = _rms(x_ref[...], g_ref[...]).astype(BF16)
    q_ref[...] = _project(xb, w_ref, 0, a) * (HEAD_DIM ** -0.5)
    k_ref[...] = _project(xb, w_ref, a, 2 * a)
    v_ref[...] = _project(xb, w_ref, 2 * a, 3 * a)
    u = _project(xb, w_ref, 3 * a + c, 3 * a + 2 * c) * _project(xb, w_ref, 3 * a + 2 * c, 3 * a + 3 * c)
    u_ref[...] = u
    tok = lax.rem(lax.broadcasted_iota(jnp.int32, u.shape, 0), ts)
    um1 = jnp.where(tok >= 1, pltpu.roll(u, 1, 0), h1_ref[...])
    um2 = jnp.where(tok >= 2, pltpu.roll(u, 2, 0), h2_ref[...])
    cb_ref[...] = _conv_branch(xb, w_ref, cw_ref, wbc_ref, u, um1, um2).astype(BF16)
    ga_ref[...] = _sigmoid(_project(xb, w_ref, 3 * a + 3 * c, 3 * a + 3 * c + d)).astype(BF16)


def _inproj_sample(x, g, w_bf, conv_w, wbc_bf, hist1, hist2, ts):
    n, d = x.shape
    out_shape = (
        jax.ShapeDtypeStruct((n, ATT_DIM), F32),
        jax.ShapeDtypeStruct((n, ATT_DIM), F32),
        jax.ShapeDtypeStruct((n, ATT_DIM), F32),
        jax.ShapeDtypeStruct((n, CONV_DIM), F32),
        jax.ShapeDtypeStruct((n, d), BF16),
        jax.ShapeDtypeStruct((n, d), BF16),
    )
    return pl.pallas_call(
        functools.partial(_inproj_sample_body, ts),
        out_shape=out_shape,
        compiler_params=pltpu.CompilerParams(vmem_limit_bytes=VMEM_LIMIT_BYTES),
        name="inproj_sample",
    )(x, g, w_bf, conv_w, wbc_bf, hist1, hist2)


def _attn_schedule(t, tq, kb):
    items = []
    for i in range(t // tq):
        jd = (i * tq) // kb
        for n in range(jd + 1):
            diag = 1 + ((i * tq) % kb) // tq if n == 0 else 0
            items.append((i * tq, (jd - n) * kb, diag, int(n == 0), int(n == jd)))
    idle = (0, 0, 0, 1, 0)
    n_steps = -(-(len(items) + PIPE) // PIPE_SLOTS) * PIPE_SLOTS
    assert items[0][3] == 1 and items[0][4] == 1
    items = [idle] * PIPE + items + [items[0]] * (n_steps - len(items))
    return np.asarray(items, np.int32).T, n_steps


def _attn_prompt_body(n_steps, tab_ref, bias_ref, q_ref, k_ref, v_ref, t_ref, o_ref,
                      z_ref, sp_ref, d_ref, a_ref, bm_ref, acc_ref, carry_ref):
    p = pl.program_id(1)
    rows, kb = bm_ref.shape[1], bm_ref.shape[2]
    tq = rows // 2
    n_diag = bm_ref.shape[0] - 1

    r = lax.broadcasted_iota(jnp.int32, (rows, kb), 0)
    col = lax.broadcasted_iota(jnp.int32, (rows, kb), 1)
    bias = jnp.where(r < tq, bias_ref[2 * p], bias_ref[2 * p + 1])
    bm_ref[0] = bias
    for dgn in range(n_diag):
        bm_ref[1 + dgn] = jnp.where(col < dgn * tq + lax.rem(r, tq), bias, MASKED_LOGIT)
    z_ref[...] = jnp.full(z_ref.shape, MASKED_LOGIT, F32)
    sp_ref[...] = jnp.zeros_like(sp_ref)
    d_ref[...] = jnp.zeros_like(d_ref)
    a_ref[...] = jnp.zeros_like(a_ref)
    acc_ref[...] = jnp.zeros_like(acc_ref)
    carry_ref[...] = jnp.zeros_like(carry_ref)
    lane = lax.broadcasted_iota(jnp.int32, (tq, HEAD_PAIR), 1)

    def logits_softplus(e, slot):
        q = q_ref[0, pl.ds(pl.multiple_of(tab_ref[0, e], tq), tq), :]
        zero = jnp.zeros_like(q)
        q2 = jnp.concatenate([jnp.where(lane < HEAD_DIM, q, zero), jnp.where(lane >= HEAD_DIM, q, zero)], axis=0)
        kblk = k_ref[0, pl.ds(pl.multiple_of(tab_ref[1, e], kb), kb), :]
        z = _dot_nt(q2, kblk) + bm_ref[tab_ref[2, e]]
        sp = _softplus(z)
        z_ref[slot] = z - sp
        sp_ref[slot] = sp.astype(BF16)

    def block_cumsum(slot):
        d_ref[slot] = _dot(sp_ref[slot], t_ref[...])

    def attention_weights(e, slot):
        carry = jnp.where(tab_ref[3, e] == 1, 0.0, carry_ref[...])
        d = d_ref[slot]
        c = d + jnp.concatenate([carry] * (kb // LANES), axis=1)
        a_ref[slot] = jnp.exp(z_ref[slot] - c).astype(BF16)
        total = d[:, 0:1] + sp_ref[slot, :, 0:1].astype(F32)
        carry_ref[...] = carry + jnp.broadcast_to(total, carry_ref.shape)

    def weights_times_values(e, slot):
        vblk = v_ref[0, pl.ds(pl.multiple_of(tab_ref[1, e], kb), kb), :]
        acc = jnp.where(tab_ref[3, e] == 1, 0.0, acc_ref[...]) + _dot(a_ref[slot], vblk)
        acc_ref[...] = acc
        o_ref[0, pl.ds(pl.multiple_of(tab_ref[0, e], tq), tq), :] = jnp.where(
            lane < HEAD_DIM, acc[:tq], acc[tq:]).astype(o_ref.dtype)

    def body(it, _):
        for u in range(PIPE_SLOTS):
            e = it * PIPE_SLOTS + u
            slot = lambda s, u=u: (u + s) % PIPE_SLOTS
            weights_times_values(e, slot(0))
            attention_weights(e + 1, slot(1))
            block_cumsum(slot(2))
            logits_softplus(e + 3, slot(3))
        return 0

    lax.fori_loop(0, n_steps // PIPE_SLOTS, body, 0)


def _attn_prompt(q, k, v, sb_bias):
    b, t, _ = q.shape
    tq = min(TQ, t)
    kb = min(KB, t)
    n_pairs = ATT_DIM // HEAD_PAIR
    tab, n_steps = _attn_schedule(t, tq, kb)
    seq_spec = pl.BlockSpec((1, t, HEAD_PAIR), lambda bi, p, tab: (bi, 0, p))
    grid_spec = pltpu.PrefetchScalarGridSpec(
        num_scalar_prefetch=1,
        grid=(b, n_pairs),
        in_specs=[pl.BlockSpec(memory_space=pltpu.SMEM), seq_spec, seq_spec, seq_spec,
                  pl.BlockSpec((kb, kb), lambda bi, p, tab: (0, 0))],
        out_specs=seq_spec,
        scratch_shapes=[pltpu.VMEM((PIPE_SLOTS, 2 * tq, kb), F32),
                        pltpu.VMEM((PIPE_SLOTS, 2 * tq, kb), BF16),
                        pltpu.VMEM((PIPE_SLOTS, 2 * tq, kb), F32),
                        pltpu.VMEM((PIPE_SLOTS, 2 * tq, kb), BF16),
                        pltpu.VMEM((1 + kb // tq, 2 * tq, kb), F32),
                        pltpu.VMEM((2 * tq, HEAD_PAIR), F32),
                        pltpu.VMEM((2 * tq, LANES), F32)],
    )
    return pl.pallas_call(
        functools.partial(_attn_prompt_body, n_steps),
        grid_spec=grid_spec,
        out_shape=jax.ShapeDtypeStruct((b, t, ATT_DIM), BF16),
        compiler_params=_params(("arbitrary", "arbitrary")),
        name="attn_prompt",
    )(jnp.asarray(tab), sb_bias, q, k, v, _tri(kb))


def _attn_sample_body(n_steps, pps, pt_ref, bias_ref, q_ref, kn_ref, vn_ref, *refs):
    del pt_ref
    k_refs = refs[:pps]
    v_refs = refs[pps:2 * pps]
    t_ref, o_ref, acc_ref, carry_ref, knew_ref, vnew_ref = refs[2 * pps:]
    j = pl.program_id(1)
    page = t_ref.shape[1]
    rows = N_HEADS * Q_PAD

    def blocks(loaders, masked):
        def side_by_side(which, h):
            return jnp.concatenate([ld[which](h) for ld in loaders], axis=1).astype(BF16)

        z = jnp.concatenate(
            [_dot(q_ref[0, h], side_by_side(0, h)) + bias_ref[h] for h in range(N_HEADS)], axis=0)
        if masked:
            r = lax.broadcasted_iota(jnp.int32, z.shape, 0)
            col = lax.broadcasted_iota(jnp.int32, z.shape, 1)
            z = jnp.where(jnp.logical_or(col >= page, col < lax.rem(r, Q_PAD)), z, MASKED_LOGIT)
        sp = _softplus(z)
        own = z - sp
        sp = sp.astype(BF16)
        carry = carry_ref[...]
        later = []
        for s in range(len(loaders)):
            d = _dot(sp[:, s * page:(s + 1) * page], t_ref[...])
            later.append(d + carry)
            total = d[:, 0:1] + sp[:, s * page:s * page + 1].astype(F32)
            carry = carry + jnp.broadcast_to(total, carry.shape)
        carry_ref[...] = carry
        ab = jnp.exp(own - jnp.concatenate(later, axis=1)).astype(BF16)
        for h in range(N_HEADS):
            acc_ref[h] += _dot_nt(ab[h * Q_PAD:(h + 1) * Q_PAD], side_by_side(1, h))

    pages = [(lambda h, kr=kr: kr[h], lambda h, vr=vr: vr[h]) for kr, vr in zip(k_refs, v_refs)]

    @pl.when(j == 0)
    def _():
        acc_ref[...] = jnp.zeros_like(acc_ref)
        carry_ref[...] = jnp.zeros_like(carry_ref)
        knew_ref[...] = jnp.zeros_like(knew_ref)
        vnew_ref[...] = jnp.zeros_like(vnew_ref)
        n_new = kn_ref.shape[3]
        knew_ref[:, :, 0:n_new] = kn_ref[0]
        vnew_ref[:, :, 0:n_new] = vn_ref[0]
        blocks([(lambda h: knew_ref[h], lambda h: vnew_ref[h])] + pages, True)

    @pl.when(j > 0)
    def _():
        blocks(pages, False)

    @pl.when(j == n_steps - 1)
    def _():
        o_ref[0] = acc_ref[...]


def _attn_sample(q, k_new, v_new, cache_k, cache_v, page_table, sb_bias):
    n_seq = q.shape[0]
    page = cache_k.shape[4]
    n_pages = page_table.shape[1]
    pps = min(PAGES_PER_STEP, n_pages)
    n_steps = n_pages // pps

    def page_spec(s):
        return pl.BlockSpec((None, None, N_HEADS, HEAD_DIM, page),
                            lambda b, j, pt: (0, pt[b, n_pages - 1 - (j * pps + s)], 0, 0, 0))

    seq4 = lambda r, c: pl.BlockSpec((1, N_HEADS, r, c), lambda b, j, pt: (b, 0, 0, 0))
    n_new = k_new.shape[3]
    grid_spec = pltpu.PrefetchScalarGridSpec(
        num_scalar_prefetch=1,
        grid=(n_seq, n_steps),
        in_specs=[pl.BlockSpec(memory_space=pltpu.SMEM), seq4(Q_PAD, HEAD_DIM),
                  seq4(HEAD_DIM, n_new), seq4(HEAD_DIM, n_new)]
                 + [page_spec(s) for s in range(pps)] * 2
                 + [pl.BlockSpec((page, page), lambda b, j, pt: (0, 0))],
        out_specs=seq4(Q_PAD, HEAD_DIM),
        scratch_shapes=[pltpu.VMEM((N_HEADS, Q_PAD, HEAD_DIM), F32),
                        pltpu.VMEM((N_HEADS * Q_PAD, LANES), F32),
                        pltpu.VMEM((N_HEADS, HEAD_DIM, page), F32),
                        pltpu.VMEM((N_HEADS, HEAD_DIM, page), F32)],
    )
    return pl.pallas_call(
        functools.partial(_attn_sample_body, n_steps, pps),
        grid_spec=grid_spec,
        out_shape=jax.ShapeDtypeStruct((n_seq, N_HEADS, Q_PAD, HEAD_DIM), F32),
        compiler_params=_params(("arbitrary", "arbitrary")),
        name="attn_sample",
    )(page_table, sb_bias, q, k_new, v_new, *([cache_k] * pps), *([cache_v] * pps), _tri(page))


def _merge_body(x_ref, att_ref, ga_ref, cb_ref, wba_ref, wout_ref, g2_ref, wrh_ref, wrl_ref, rb_ref,
                h_ref, xn_ref, gate_ref):
    merged = ga_ref[...].astype(F32) * _dot(att_ref[...], wba_ref[...]) + cb_ref[...].astype(F32)
    h = x_ref[...] + _dot(merged.astype(BF16), wout_ref[...])
    h_ref[...] = h
    xn = _rms(h, g2_ref[...])
    xh, xl = _split_bf16(xn)
    xn_ref[...] = xh
    wrh = wrh_ref[...]
    logits = _dot_nt(wrh, xh) + _dot_nt(wrh, xl) + _dot_nt(wrl_ref[...], xh)
    scores = _sigmoid(logits)
    work = scores + rb_ref[...]
    n_exp = work.shape[0]
    expert = lax.broadcasted_iota(jnp.int32, work.shape, 0)
    chosen = jnp.zeros(work.shape, jnp.bool_)
    for _ in range(TOP_K):
        m = jnp.max(work, axis=0, keepdims=True)
        first = jnp.min(jnp.where(work == m, expert, n_exp), axis=0, keepdims=True)
        sel = expert == first
        chosen = jnp.logical_or(chosen, sel)
        work = jnp.where(sel, -jnp.inf, work)
    picked = jnp.where(chosen, scores, 0.0)
    gate = picked / jnp.sum(picked, axis=0, keepdims=True) * ROUTED_SCALE
    gate_ref[...] = gate.T


def _merge(x, att, ga, cb, wba_bf, wout_bf, g2, wr_hi, wr_lo, rbias):
    n, d = x.shape
    tm = min(TM_PROJ, n)
    n_exp = wr_hi.shape[0]
    tok = lambda w: pl.BlockSpec((tm, w), lambda i: (i, 0))
    full = lambda shp: pl.BlockSpec(shp, lambda i: (0,) * len(shp))
    return pl.pallas_call(
        _merge_body,
        grid=(n // tm,),
        in_specs=[tok(d), tok(ATT_DIM), tok(d), tok(d), full((ATT_DIM, d)), full((d, d)), full((1, d)),
                  full((n_exp, d)), full((n_exp, d)), full((n_exp, 1))],
        out_specs=(tok(d), tok(d), tok(n_exp)),
        out_shape=(jax.ShapeDtypeStruct((n, d), F32), jax.ShapeDtypeStruct((n, d), BF16),
                   jax.ShapeDtypeStruct((n, n_exp), F32)),
        compiler_params=_params(("arbitrary",)),
        name="merge_router",
    )(x, att, ga, cb, wba_bf, wout_bf, g2, wr_hi, wr_lo, rbias)


def _swiglu_hidden(x, wg, wu):
    a = _dot(x, wg)
    return (a * _sigmoid(a)) * _dot(x, wu)


def _moe_body(xn_ref, h_ref, gate_ref, wg_ref, wu_ref, wd_ref, wgs_ref, wus_ref, wds_ref, gf_ref, y_ref):
    s = pl.program_id(1)
    x = xn_ref[...]

    @pl.when(s == 0)
    def _():
        y_ref[...] = _dot(_swiglu_hidden(x, wgs_ref[...], wus_ref[...]).astype(BF16), wds_ref[...])

    gate = gate_ref[...]
    lane = lax.broadcasted_iota(jnp.int32, gate.shape, 1)
    hidden = []
    for j in range(wg_ref.shape[0]):
        g_e = jnp.sum(jnp.where(lane == s * wg_ref.shape[0] + j, gate, 0.0), axis=-1, keepdims=True)
        hidden.append((g_e * _swiglu_hidden(x, wg_ref[j], wu_ref[j])).astype(BF16))
    y_ref[...] += _dot(jnp.concatenate(hidden, axis=1), wd_ref[...])

    @pl.when(s == pl.num_programs(1) - 1)
    def _():
        y_ref[...] = _rms(h_ref[...] + y_ref[...], gf_ref[...])


def _token_tile(n, largest):
    for tm in range(largest - largest % 16, 0, -16):
        if n % tm == 0:
            return tm
    return n


def _moe(xn, h, gate, wg_bf, wu_bf, wd_bf, wgs_bf, wus_bf, wds_bf, gf):
    n, d = h.shape
    tm = _token_tile(n, TM_MOE)
    n_exp, _, de = wg_bf.shape
    per = EXPERTS_PER_STEP
    assert n_exp % per == 0
    ds = wgs_bf.shape[1]
    tok = lambda w: pl.BlockSpec((tm, w), lambda i, s: (i, 0))
    full = lambda shp: pl.BlockSpec(shp, lambda i, s: (0,) * len(shp))
    return pl.pallas_call(
        _moe_body,
        grid=(n // tm, n_exp // per),
        in_specs=[tok(d), tok(d), tok(n_exp),
                  pl.BlockSpec((per, d, de), lambda i, s: (s, 0, 0)),
                  pl.BlockSpec((per, d, de), lambda i, s: (s, 0, 0)),
                  pl.BlockSpec((per * de, d), lambda i, s: (s, 0)),
                  full((d, ds)), full((d, ds)), full((ds, d)), full((1, d))],
        out_specs=tok(d),
        out_shape=jax.ShapeDtypeStruct((n, d), F32),
        compiler_params=_params(("arbitrary", "arbitrary")),
        name="moe",
    )(xn, h, gate, wg_bf, wu_bf, wd_bf.reshape(n_exp * de, d), wgs_bf, wus_bf, wds_bf, gf)


def kernel(x_prompt, x_sample, cache_k, cache_v, state_conv, page_table, norm_mix_g, w_in, sb_bias, conv_w,
           w_branch_att, w_branch_conv, w_out, norm_ffn_g, w_router, router_bias, w_gate_e, w_up_e, w_down_e,
           w_gate_s, w_up_s, w_down_s, norm_final_g):
    depth = w_in.shape[0]
    assert depth == 1, "single trunk layer"
    b, t, d = x_prompt.shape
    db, ts, _ = x_sample.shape
    assert ts <= 8 and (CONV_DIM, ATT_DIM) == (state_conv.shape[-1], cache_k.shape[-2] * cache_k.shape[-1])
    l = 0
    row = lambda g: g.reshape(1, -1)
    w_in_bf = w_in[l].astype(BF16)
    wbc_bf = w_branch_conv[l].astype(BF16)
    wba_bf = w_branch_att[l].astype(BF16)
    wout_bf = w_out[l].astype(BF16)
    wr_t = w_router[l].T
    wr_hi = wr_t.astype(BF16)
    wr_lo = (wr_t - wr_hi.astype(F32)).astype(BF16)
    moe_w = tuple(w[l].astype(BF16) for w in (w_gate_e, w_up_e, w_down_e, w_gate_s, w_up_s, w_down_s))
    bias = sb_bias[l]

    def channel_mixer(x2d, att, ga, cb):
        h, xn, gate = _merge(x2d, att, ga, cb, wba_bf, wout_bf, row(norm_ffn_g[l]), wr_hi, wr_lo,
                             router_bias[l].reshape(-1, 1))
        return _moe(xn, h, gate, *moe_w, row(norm_final_g))

    q, k_bf, v_bf, k_f, v_f, ga, cb, conv_tail = _inproj_prompt(
        x_prompt, row(norm_mix_g[l]), w_in_bf, conv_w[l], wbc_bf)
    att = _attn_prompt(q, k_bf, v_bf, bias)
    n = b * t
    y_prompt = channel_mixer(x_prompt.reshape(n, d), att.reshape(n, ATT_DIM), ga.reshape(n, d),
                             cb.reshape(n, d)).reshape(b, t, d)

    ns = db * ts
    st = state_conv[l]
    n_hist = st.shape[1]
    hist1 = jnp.broadcast_to(st[:, n_hist - 1:n_hist], (db, ts, CONV_DIM)).reshape(ns, CONV_DIM)
    hist2 = jnp.concatenate([st, jnp.zeros((db, ts - n_hist, CONV_DIM), F32)], axis=1).reshape(ns, CONV_DIM)
    q_s, k_s, v_s, u_s, ga_s, cb_s = _inproj_sample(
        x_sample.reshape(ns, d), row(norm_mix_g[l]), w_in_bf, conv_w[l], wbc_bf, hist1, hist2, ts)

    by_head = lambda a: a.reshape(db, ts, N_HEADS, HEAD_DIM)
    q_h = jnp.pad(by_head(q_s).transpose(0, 2, 1, 3), ((0, 0), (0, 0), (0, Q_PAD - ts), (0, 0))).astype(BF16)
    dim_major = lambda c: c.transpose(0, 1, 3, 4, 2)
    att_s = _attn_sample(q_h, by_head(k_s).transpose(0, 2, 3, 1), by_head(v_s).transpose(0, 2, 3, 1),
                         dim_major(cache_k), dim_major(cache_v), page_table, bias)
    att_s = att_s[:, :, :ts].transpose(0, 2, 1, 3).reshape(ns, ATT_DIM).astype(BF16)
    y_sample = channel_mixer(x_sample.reshape(ns, d), att_s, ga_s, cb_s).reshape(db, ts, d)

    heads = lambda a, lead: a.reshape(1, *lead, N_HEADS, HEAD_DIM)
    return (y_prompt, y_sample,
            heads(k_f, (b, t)), heads(v_f, (b, t)), conv_tail[:, 6:8].reshape(1, b, 2, CONV_DIM),
            heads(k_s, (db, ts)), heads(v_s, (db, ts)),
            u_s.reshape(db, ts, CONV_DIM)[:, ts - 2:].reshape(1, db, 2, CONV_DIM))
```

```python
import functools

import jax
import jax.numpy as jnp
import numpy as np
from jax import lax
from jax.experimental import pallas as pl
from jax.experimental.pallas import tpu as pltpu

F32 = jnp.float32
BF16 = jnp.bfloat16

N_HEADS = 8
HEAD_DIM = 64
ATT_DIM = N_HEADS * HEAD_DIM
CONV_DIM = 512
TOP_K = 8
ROUTED_SCALE = 2.5
EPS = 1e-6

VMEM_LIMIT_BYTES = 56 * 1024 * 1024
LANES = 128
HEAD_PAIR = LANES

TM_PROJ = 512
TM_MOE = 1024
EXPERTS_PER_STEP = 4
TQ = 128
KB = 256
Q_PAD = 16
PAGES_PER_STEP = 16
PIPE = 4
PIPE_SLOTS = PIPE + 1
MASKED_LOGIT = -1e30
SOFTPLUS_CLAMP = 80.0


def _params(sem):
    return pltpu.CompilerParams(dimension_semantics=sem, vmem_limit_bytes=VMEM_LIMIT_BYTES)


def _rms(x, g):
    return (x * lax.rsqrt(jnp.mean(x * x, axis=-1, keepdims=True) + EPS)) * g


def _sigmoid(x):
    return 1.0 / (1.0 + jnp.exp(-x))


def _softplus(z):
    return jnp.maximum(z, jnp.log(1.0 + jnp.exp(jnp.minimum(z, SOFTPLUS_CLAMP))))


def _dot(a, b):
    return jnp.dot(a, b, preferred_element_type=F32)


def _dot_nt(a, b):
    return lax.dot_general(a, b, (((1,), (1,)), ((), ())), preferred_element_type=F32)


def _split_bf16(x):
    hi = x.astype(BF16)
    lo = (x - hi.astype(F32)).astype(BF16)
    return hi, lo


def _tri(n):
    j = lax.broadcasted_iota(jnp.int32, (n, n), 0)
    s = lax.broadcasted_iota(jnp.int32, (n, n), 1)
    return (j > s).astype(BF16)


def _project(xb, w_ref, lo, hi):
    return _dot(xb, w_ref[:, lo:hi])


def _conv_branch(xb, w_ref, cw_ref, wbc_ref, u, um1, um2):
    a = ATT_DIM
    cw = cw_ref[...]
    conv = _project(xb, w_ref, 3 * a, 3 * a + CONV_DIM) * (cw[0:1] * um2 + cw[1:2] * um1 + cw[2:3] * u)
    d = w_ref.shape[0]
    g_conv = _project(xb, w_ref, 3 * a + 3 * CONV_DIM + d, 3 * a + 3 * CONV_DIM + 2 * d)
    return _sigmoid(g_conv) * _dot(conv.astype(BF16), wbc_ref[...])


def _inproj_prompt_body(x_ref, g_ref, w_ref, cw_ref, wbc_ref,
                        q_ref, kb_ref, vb_ref, kf_ref, vf_ref, ga_ref, cb_ref, cs_ref, hist_ref):
    a, c, d = ATT_DIM, CONV_DIM, x_ref.shape[2]
    tm = x_ref.shape[1]
    xb = _rms(x_ref[0], g_ref[...]).astype(BF16)
    q_ref[0] = (_project(xb, w_ref, 0, a) * (HEAD_DIM ** -0.5)).astype(BF16)
    k = _project(xb, w_ref, a, 2 * a)
    kf_ref[0] = k
    kb_ref[0] = k.astype(BF16)
    v = _project(xb, w_ref, 2 * a, 3 * a)
    vf_ref[0] = v
    vb_ref[0] = v.astype(BF16)
    u = _project(xb, w_ref, 3 * a + c, 3 * a + 2 * c) * _project(xb, w_ref, 3 * a + 2 * c, 3 * a + 3 * c)

    @pl.when(pl.program_id(1) == 0)
    def _():
        hist_ref[...] = jnp.zeros_like(hist_ref)

    row = lax.broadcasted_iota(jnp.int32, u.shape, 0)
    h0 = hist_ref[6:7, :]
    h1 = hist_ref[7:8, :]
    um1 = jnp.where(row < 1, h1, pltpu.roll(u, 1, 0))
    um2 = jnp.where(row < 1, h0, jnp.where(row < 2, h1, pltpu.roll(u, 2, 0)))
    cb_ref[0] = _conv_branch(xb, w_ref, cw_ref, wbc_ref, u, um1, um2).astype(BF16)
    tail = u[tm - 8:tm, :]
    hist_ref[...] = tail
    cs_ref[0] = tail
    ga_ref[0] = _sigmoid(_project(xb, w_ref, 3 * a + 3 * c, 3 * a + 3 * c + d)).astype(BF16)


def _inproj_prompt(x, g, w_bf, conv_w, wbc_bf):
    b, t, d = x.shape
    tm = min(TM_PROJ, t)
    nt = t // tm
    n_in = w_bf.shape[1]
    tok = lambda n: pl.BlockSpec((1, tm, n), lambda i, j: (i, j, 0))
    full = lambda shp: pl.BlockSpec(shp, lambda i, j: (0,) * len(shp))
    out_shape = (
        jax.ShapeDtypeStruct((b, t, ATT_DIM), BF16),
        jax.ShapeDtypeStruct((b, t, ATT_DIM), BF16),
        jax.ShapeDtypeStruct((b, t, ATT_DIM), BF16),
        jax.ShapeDtypeStruct((b, t, ATT_DIM), F32),
        jax.ShapeDtypeStruct((b, t, ATT_DIM), F32),
        jax.ShapeDtypeStruct((b, t, d), BF16),
        jax.ShapeDtypeStruct((b, t, d), BF16),
        jax.ShapeDtypeStruct((b, 8, CONV_DIM), F32),
    )
    return pl.pallas_call(
        _inproj_prompt_body,
        grid=(b, nt),
        in_specs=[tok(d), full((1, d)), full((d, n_in)), full((3, CONV_DIM)), full((CONV_DIM, d))],
        out_specs=(tok(ATT_DIM), tok(ATT_DIM), tok(ATT_DIM), tok(ATT_DIM), tok(ATT_DIM), tok(d), tok(d),
                   pl.BlockSpec((1, 8, CONV_DIM), lambda i, j: (i, 0, 0))),
        out_shape=out_shape,
        scratch_shapes=[pltpu.VMEM((8, CONV_DIM), F32)],
        compiler_params=_params(("arbitrary", "arbitrary")),
        name="inproj_prompt",
    )(x, g, w_bf, conv_w, wbc_bf)


def _inproj_sample_body(ts, x_ref, g_ref, w_ref, cw_ref, wbc_ref, h1_ref, h2_ref,
                        q_ref, k_ref, v_ref, u_ref, ga_ref, cb_ref):
    a, c, d = ATT_DIM, CONV_DIM, x_ref.shape[1]
    xb = _rms(x_ref[...], g_ref[...]).astype(BF16)
    q_ref[...] = _project(xb, w_ref, 0, a) * (HEAD_DIM ** -0.5)
    k_ref[...] = _project(xb, w_ref, a, 2 * a)
    v_ref[...] = _project(xb, w_ref, 2 * a, 3 * a)
    u = _project(xb, w_ref, 3 * a + c, 3 * a + 2 * c) * _project(xb, w_ref, 3 * a + 2 * c, 3 * a + 3 * c)
    u_ref[...] = u
    tok = lax.rem(lax.broadcasted_iota(jnp.int32, u.shape, 0), ts)
    um1 = jnp.where(tok >= 1, pltpu.roll(u, 1, 0), h1_ref[...])
    um2 = jnp.where(tok >= 2, pltpu.roll(u, 2, 0), h2_ref[...])
    cb_ref[...] = _conv_branch(xb, w_ref, cw_ref, wbc_ref, u, um1, um2).astype(BF16)
    ga_ref[...] = _sigmoid(_project(xb, w_ref, 3 * a + 3 * c, 3 * a + 3 * c + d)).astype(BF16)


def _inproj_sample(x, g, w_bf, conv_w, wbc_bf, hist1, hist2, ts):
    n, d = x.shape
    out_shape = (
        jax.ShapeDtypeStruct((n, ATT_DIM), F32),
        jax.ShapeDtypeStruct((n, ATT_DIM), F32),
        jax.ShapeDtypeStruct((n, ATT_DIM), F32),
        jax.ShapeDtypeStruct((n, CONV_DIM), F32),
        jax.ShapeDtypeStruct((n, d), BF16),
        jax.ShapeDtypeStruct((n, d), BF16),
    )
    return pl.pallas_call(
        functools.partial(_inproj_sample_body, ts),
        out_shape=out_shape,
        compiler_params=pltpu.CompilerParams(vmem_limit_bytes=VMEM_LIMIT_BYTES),
        name="inproj_sample",
    )(x, g, w_bf, conv_w, wbc_bf, hist1, hist2)


def _attn_schedule(t, tq, kb):
    items = []
    for i in range(t // tq):
        jd = (i * tq) // kb
        for n in range(jd + 1):
            diag = 1 + ((i * tq) % kb) // tq if n == 0 else 0
            items.append((i * tq, (jd - n) * kb, diag, int(n == 0), int(n == jd)))
    idle = (0, 0, 0, 1, 0)
    n_steps = -(-(len(items) + PIPE) // PIPE_SLOTS) * PIPE_SLOTS
    assert items[0][3] == 1 and items[0][4] == 1
    items = [idle] * PIPE + items + [items[0]] * (n_steps - len(items))
    return np.asarray(items, np.int32).T, n_steps


def _attn_prompt_body(n_steps, tab_ref, bias_ref, q_ref, k_ref, v_ref, t_ref, o_ref,
                      z_ref, sp_ref, d_ref, a_ref, bm_ref, acc_ref, carry_ref):
    p = pl.program_id(1)
    rows, kb = bm_ref.shape[1], bm_ref.shape[2]
    tq = rows // 2
    n_diag = bm_ref.shape[0] - 1

    r = lax.broadcasted_iota(jnp.int32, (rows, kb), 0)
    col = lax.broadcasted_iota(jnp.int32, (rows, kb), 1)
    bias = jnp.where(r < tq, bias_ref[2 * p], bias_ref[2 * p + 1])
    bm_ref[0] = bias
    for dgn in range(n_diag):
        bm_ref[1 + dgn] = jnp.where(col < dgn * tq + lax.rem(r, tq), bias, MASKED_LOGIT)
    z_ref[...] = jnp.full(z_ref.shape, MASKED_LOGIT, F32)
    sp_ref[...] = jnp.zeros_like(sp_ref)
    d_ref[...] = jnp.zeros_like(d_ref)
    a_ref[...] = jnp.zeros_like(a_ref)
    acc_ref[...] = jnp.zeros_like(acc_ref)
    carry_ref[...] = jnp.zeros_like(carry_ref)
    lane = lax.broadcasted_iota(jnp.int32, (tq, HEAD_PAIR), 1)

    def logits(e, slot):
        q = q_ref[0, pl.ds(pl.multiple_of(tab_ref[0, e], tq), tq), :]
        zero = jnp.zeros_like(q)
        q2 = jnp.concatenate([jnp.where(lane < HEAD_DIM, q, zero), jnp.where(lane >= HEAD_DIM, q, zero)], axis=0)
        kblk = k_ref[0, pl.ds(pl.multiple_of(tab_ref[1, e], kb), kb), :]
        z_ref[slot] = _dot_nt(q2, kblk) + bm_ref[tab_ref[2, e]]

    def softplus(slot):
        z = z_ref[slot]
        sp = _softplus(z)
        z_ref[slot] = z - sp
        sp_ref[slot] = sp.astype(BF16)

    def block_cumsum(slot):
        d_ref[slot] = _dot(sp_ref[slot], t_ref[...])

    def attention_weights(e, slot):
        carry = jnp.where(tab_ref[3, e] == 1, 0.0, carry_ref[...])
        d = d_ref[slot]
        c = d + jnp.concatenate([carry] * (kb // LANES), axis=1)
        a_ref[slot] = jnp.exp(z_ref[slot] - c).astype(BF16)
        total = d[:, 0:1] + sp_ref[slot, :, 0:1].astype(F32)
        carry_ref[...] = carry + jnp.broadcast_to(total, carry_ref.shape)

    def weights_times_values(e, slot):
        vblk = v_ref[0, pl.ds(pl.multiple_of(tab_ref[1, e], kb), kb), :]
        acc = jnp.where(tab_ref[3, e] == 1, 0.0, acc_ref[...]) + _dot(a_ref[slot], vblk)
        acc_ref[...] = acc
        o_ref[0, pl.ds(pl.multiple_of(tab_ref[0, e], tq), tq), :] = jnp.where(
            lane < HEAD_DIM, acc[:tq], acc[tq:]).astype(o_ref.dtype)

    def body(it, _):
        for u in range(PIPE_SLOTS):
            e = it * PIPE_SLOTS + u
            slot = lambda s, u=u: (u + s) % PIPE_SLOTS
            weights_times_values(e, slot(0))
            attention_weights(e + 1, slot(1))
            block_cumsum(slot(2))
            softplus(slot(3))
            logits(e + 4, slot(4))
        return 0

    lax.fori_loop(0, n_steps // PIPE_SLOTS, body, 0)


def _attn_prompt(q, k, v, sb_bias):
    b, t, _ = q.shape
    tq = min(TQ, t)
    kb = min(KB, t)
    n_pairs = ATT_DIM // HEAD_PAIR
    tab, n_steps = _attn_schedule(t, tq, kb)
    seq_spec = pl.BlockSpec((1, t, HEAD_PAIR), lambda bi, p, tab: (bi, 0, p))
    grid_spec = pltpu.PrefetchScalarGridSpec(
        num_scalar_prefetch=1,
        grid=(b, n_pairs),
        in_specs=[pl.BlockSpec(memory_space=pltpu.SMEM), seq_spec, seq_spec, seq_spec,
                  pl.BlockSpec((kb, kb), lambda bi, p, tab: (0, 0))],
        out_specs=seq_spec,
        scratch_shapes=[pltpu.VMEM((PIPE_SLOTS, 2 * tq, kb), F32),
                        pltpu.VMEM((PIPE_SLOTS, 2 * tq, kb), BF16),
                        pltpu.VMEM((PIPE_SLOTS, 2 * tq, kb), F32),
                        pltpu.VMEM((PIPE_SLOTS, 2 * tq, kb), BF16),
                        pltpu.VMEM((1 + kb // tq, 2 * tq, kb), F32),
                        pltpu.VMEM((2 * tq, HEAD_PAIR), F32),
                        pltpu.VMEM((2 * tq, LANES), F32)],
    )
    return pl.pallas_call(
        functools.partial(_attn_prompt_body, n_steps),
        grid_spec=grid_spec,
        out_shape=jax.ShapeDtypeStruct((b, t, ATT_DIM), BF16),
        compiler_params=_params(("arbitrary", "arbitrary")),
        name="attn_prompt",
    )(jnp.asarray(tab), sb_bias, q, k, v, _tri(kb))


def _attn_sample_body(n_steps, pps, pt_ref, bias_ref, q_ref, kn_ref, vn_ref, *refs):
    del pt_ref
    k_refs = refs[:pps]
    v_refs = refs[pps:2 * pps]
    t_ref, o_ref, acc_ref, carry_ref, knew_ref, vnew_ref = refs[2 * pps:]
    j = pl.program_id(1)
    page = t_ref.shape[1]
    rows = N_HEADS * Q_PAD

    def blocks(loaders, masked):
        def side_by_side(which, h):
            return jnp.concatenate([ld[which](h) for ld in loaders], axis=1).astype(BF16)

        z = jnp.concatenate(
            [_dot(q_ref[0, h], side_by_side(0, h)) + bias_ref[h] for h in range(N_HEADS)], axis=0)
        if masked:
            r = lax.broadcasted_iota(jnp.int32, z.shape, 0)
            col = lax.broadcasted_iota(jnp.int32, z.shape, 1)
            z = jnp.where(jnp.logical_or(col >= page, col < lax.rem(r, Q_PAD)), z, MASKED_LOGIT)
        sp = _softplus(z)
        own = z - sp
        sp = sp.astype(BF16)
        carry = carry_ref[...]
        later = []
        for s in range(len(loaders)):
            d = _dot(sp[:, s * page:(s + 1) * page], t_ref[...])
            later.append(d + carry)
            total = d[:, 0:1] + sp[:, s * page:s * page + 1].astype(F32)
            carry = carry + jnp.broadcast_to(total, carry.shape)
        carry_ref[...] = carry
        ab = jnp.exp(own - jnp.concatenate(later, axis=1)).astype(BF16)
        for h in range(N_HEADS):
            acc_ref[h] += _dot_nt(ab[h * Q_PAD:(h + 1) * Q_PAD], side_by_side(1, h))

    pages = [(lambda h, kr=kr: kr[h], lambda h, vr=vr: vr[h]) for kr, vr in zip(k_refs, v_refs)]

    @pl.when(j == 0)
    def _():
        acc_ref[...] = jnp.zeros_like(acc_ref)
        carry_ref[...] = jnp.zeros_like(carry_ref)
        knew_ref[...] = jnp.zeros_like(knew_ref)
        vnew_ref[...] = jnp.zeros_like(vnew_ref)
        n_new = kn_ref.shape[3]
        knew_ref[:, :, 0:n_new] = kn_ref[0]
        vnew_ref[:, :, 0:n_new] = vn_ref[0]
        blocks([(lambda h: knew_ref[h], lambda h: vnew_ref[h])] + pages, True)

    @pl.when(j > 0)
    def _():
        blocks(pages, False)

    @pl.when(j == n_steps - 1)
    def _():
        o_ref[0] = acc_ref[...]


def _attn_sample(q, k_new, v_new, cache_k, cache_v, page_table, sb_bias):
    n_seq = q.shape[0]
    page = cache_k.shape[4]
    n_pages = page_table.shape[1]
    pps = min(PAGES_PER_STEP, n_pages)
    n_steps = n_pages // pps

    def page_spec(s):
        return pl.BlockSpec((None, None, N_HEADS, HEAD_DIM, page),
                            lambda b, j, pt: (0, pt[b, n_pages - 1 - (j * pps + s)], 0, 0, 0))

    seq4 = lambda r, c: pl.BlockSpec((1, N_HEADS, r, c), lambda b, j, pt: (b, 0, 0, 0))
    n_new = k_new.shape[3]
    grid_spec = pltpu.PrefetchScalarGridSpec(
        num_scalar_prefetch=1,
        grid=(n_seq, n_steps),
        in_specs=[pl.BlockSpec(memory_space=pltpu.SMEM), seq4(Q_PAD, HEAD_DIM),
                  seq4(HEAD_DIM, n_new), seq4(HEAD_DIM, n_new)]
                 + [page_spec(s) for s in range(pps)] * 2
                 + [pl.BlockSpec((page, page), lambda b, j, pt: (0, 0))],
        out_specs=seq4(Q_PAD, HEAD_DIM),
        scratch_shapes=[pltpu.VMEM((N_HEADS, Q_PAD, HEAD_DIM), F32),
                        pltpu.VMEM((N_HEADS * Q_PAD, LANES), F32),
                        pltpu.VMEM((N_HEADS, HEAD_DIM, page), F32),
                        pltpu.VMEM((N_HEADS, HEAD_DIM, page), F32)],
    )
    return pl.pallas_call(
        functools.partial(_attn_sample_body, n_steps, pps),
        grid_spec=grid_spec,
        out_shape=jax.ShapeDtypeStruct((n_seq, N_HEADS, Q_PAD, HEAD_DIM), F32),
        compiler_params=_params(("arbitrary", "arbitrary")),
        name="attn_sample",
    )(page_table, sb_bias, q, k_new, v_new, *([cache_k] * pps), *([cache_v] * pps), _tri(page))


def _merge_body(x_ref, att_ref, ga_ref, cb_ref, wba_ref, wout_ref, g2_ref, wrh_ref, wrl_ref, rb_ref,
                h_ref, xn_ref, gate_ref):
    merged = ga_ref[...].astype(F32) * _dot(att_ref[...], wba_ref[...]) + cb_ref[...].astype(F32)
    h = x_ref[...] + _dot(merged.astype(BF16), wout_ref[...])
    h_ref[...] = h
    xn = _rms(h, g2_ref[...])
    xh, xl = _split_bf16(xn)
    xn_ref[...] = xh
    wrh = wrh_ref[...]
    logits = _dot_nt(wrh, xh) + _dot_nt(wrh, xl) + _dot_nt(wrl_ref[...], xh)
    scores = _sigmoid(logits)
    work = scores + rb_ref[...]
    n_exp = work.shape[0]
    expert = lax.broadcasted_iota(jnp.int32, work.shape, 0)
    chosen = jnp.zeros(work.shape, jnp.bool_)
    for _ in range(TOP_K):
        m = jnp.max(work, axis=0, keepdims=True)
        first = jnp.min(jnp.where(work == m, expert, n_exp), axis=0, keepdims=True)
        sel = expert == first
        chosen = jnp.logical_or(chosen, sel)
        work = jnp.where(sel, -jnp.inf, work)
    picked = jnp.where(chosen, scores, 0.0)
    gate = picked / jnp.sum(picked, axis=0, keepdims=True) * ROUTED_SCALE
    gate_ref[...] = gate.T


def _merge(x, att, ga, cb, wba_bf, wout_bf, g2, wr_hi, wr_lo, rbias):
    n, d = x.shape
    tm = min(TM_PROJ, n)
    n_exp = wr_hi.shape[0]
    tok = lambda w: pl.BlockSpec((tm, w), lambda i: (i, 0))
    full = lambda shp: pl.BlockSpec(shp, lambda i: (0,) * len(shp))
    return pl.pallas_call(
        _merge_body,
        grid=(n // tm,),
        in_specs=[tok(d), tok(ATT_DIM), tok(d), tok(d), full((ATT_DIM, d)), full((d, d)), full((1, d)),
                  full((n_exp, d)), full((n_exp, d)), full((n_exp, 1))],
        out_specs=(tok(d), tok(d), tok(n_exp)),
        out_shape=(jax.ShapeDtypeStruct((n, d), F32), jax.ShapeDtypeStruct((n, d), BF16),
                   jax.ShapeDtypeStruct((n, n_exp), F32)),
        compiler_params=_params(("arbitrary",)),
        name="merge_router",
    )(x, att, ga, cb, wba_bf, wout_bf, g2, wr_hi, wr_lo, rbias)


def _swiglu_hidden(x, wg, wu):
    a = _dot(x, wg)
    return (a * _sigmoid(a)) * _dot(x, wu)


def _moe_body(xn_ref, h_ref, gate_ref, wg_ref, wu_ref, wd_ref, wgs_ref, wus_ref, wds_ref, gf_ref, y_ref):
    s = pl.program_id(1)
    x = xn_ref[...]

    @pl.when(s == 0)
    def _():
        y_ref[...] = _dot(_swiglu_hidden(x, wgs_ref[...], wus_ref[...]).astype(BF16), wds_ref[...])

    gate = gate_ref[...]
    lane = lax.broadcasted_iota(jnp.int32, gate.shape, 1)
    hidden = []
    for j in range(wg_ref.shape[0]):
        g_e = jnp.sum(jnp.where(lane == s * wg_ref.shape[0] + j, gate, 0.0), axis=-1, keepdims=True)
        hidden.append((g_e * _swiglu_hidden(x, wg_ref[j], wu_ref[j])).astype(BF16))
    y_ref[...] += _dot(jnp.concatenate(hidden, axis=1), wd_ref[...])

    @pl.when(s == pl.num_programs(1) - 1)
    def _():
        y_ref[...] = _rms(h_ref[...] + y_ref[...], gf_ref[...])


def _token_tile(n, largest):
    for tm in range(largest - largest % 16, 0, -16):
        if n % tm == 0:
            return tm
    return n


def _moe(xn, h, gate, wg_bf, wu_bf, wd_bf, wgs_bf, wus_bf, wds_bf, gf):
    n, d = h.shape
    tm = _token_tile(n, TM_MOE)
    n_exp, _, de = wg_bf.shape
    per = EXPERTS_PER_STEP
    assert n_exp % per == 0
    ds = wgs_bf.shape[1]
    tok = lambda w: pl.BlockSpec((tm, w), lambda i, s: (i, 0))
    full = lambda shp: pl.BlockSpec(shp, lambda i, s: (0,) * len(shp))
    return pl.pallas_call(
        _moe_body,
        grid=(n // tm, n_exp // per),
        in_specs=[tok(d), tok(d), tok(n_exp),
                  pl.BlockSpec((per, d, de), lambda i, s: (s, 0, 0)),
                  pl.BlockSpec((per, d, de), lambda i, s: (s, 0, 0)),
                  pl.BlockSpec((per * de, d), lambda i, s: (s, 0)),
                  full((d, ds)), full((d, ds)), full((ds, d)), full((1, d))],
        out_specs=tok(d),
        out_shape=jax.ShapeDtypeStruct((n, d), F32),
        compiler_params=_params(("arbitrary", "arbitrary")),
        name="moe",
    )(xn, h, gate, wg_bf, wu_bf, wd_bf.reshape(n_exp * de, d), wgs_bf, wus_bf, wds_bf, gf)


def kernel(x_prompt, x_sample, cache_k, cache_v, state_conv, page_table, norm_mix_g, w_in, sb_bias, conv_w,
           w_branch_att, w_branch_conv, w_out, norm_ffn_g, w_router, router_bias, w_gate_e, w_up_e, w_down_e,
           w_gate_s, w_up_s, w_down_s, norm_final_g):
    depth = w_in.shape[0]
    assert depth == 1, "single trunk layer"
    b, t, d = x_prompt.shape
    db, ts, _ = x_sample.shape
    assert ts <= 8 and (CONV_DIM, ATT_DIM) == (state_conv.shape[-1], cache_k.shape[-2] * cache_k.shape[-1])
    l = 0
    row = lambda g: g.reshape(1, -1)
    w_in_bf = w_in[l].astype(BF16)
    wbc_bf = w_branch_conv[l].astype(BF16)
    wba_bf = w_branch_att[l].astype(BF16)
    wout_bf = w_out[l].astype(BF16)
    wr_t = w_router[l].T
    wr_hi = wr_t.astype(BF16)
    wr_lo = (wr_t - wr_hi.astype(F32)).astype(BF16)
    moe_w = tuple(w[l].astype(BF16) for w in (w_gate_e, w_up_e, w_down_e, w_gate_s, w_up_s, w_down_s))
    bias = sb_bias[l]

    def channel_mixer(x2d, att, ga, cb):
        h, xn, gate = _merge(x2d, att, ga, cb, wba_bf, wout_bf, row(norm_ffn_g[l]), wr_hi, wr_lo,
                             router_bias[l].reshape(-1, 1))
        return _moe(xn, h, gate, *moe_w, row(norm_final_g))

    q, k_bf, v_bf, k_f, v_f, ga, cb, conv_tail = _inproj_prompt(
        x_prompt, row(norm_mix_g[l]), w_in_bf, conv_w[l], wbc_bf)
    att = _attn_prompt(q, k_bf, v_bf, bias)
    n = b * t
    y_prompt = channel_mixer(x_prompt.reshape(n, d), att.reshape(n, ATT_DIM), ga.reshape(n, d),
                             cb.reshape(n, d)).reshape(b, t, d)

    ns = db * ts
    st = state_conv[l]
    n_hist = st.shape[1]
    hist1 = jnp.broadcast_to(st[:, n_hist - 1:n_hist], (db, ts, CONV_DIM)).reshape(ns, CONV_DIM)
    hist2 = jnp.concatenate([st, jnp.zeros((db, ts - n_hist, CONV_DIM), F32)], axis=1).reshape(ns, CONV_DIM)
    q_s, k_s, v_s, u_s, ga_s, cb_s = _inproj_sample(
        x_sample.reshape(ns, d), row(norm_mix_g[l]), w_in_bf, conv_w[l], wbc_bf, hist1, hist2, ts)

    by_head = lambda a: a.reshape(db, ts, N_HEADS, HEAD_DIM)
    q_h = jnp.pad(by_head(q_s).transpose(0, 2, 1, 3), ((0, 0), (0, 0), (0, Q_PAD - ts), (0, 0))).astype(BF16)
    dim_major = lambda c: c.transpose(0, 1, 3, 4, 2)
    att_s = _attn_sample(q_h, by_head(k_s).transpose(0, 2, 3, 1), by_head(v_s).transpose(0, 2, 3, 1),
                         dim_major(cache_k), dim_major(cache_v), page_table, bias)
    att_s = att_s[:, :, :ts].transpose(0, 2, 1, 3).reshape(ns, ATT_DIM).astype(BF16)
    y_sample = channel_mixer(x_sample.reshape(ns, d), att_s, ga_s, cb_s).reshape(db, ts, d)

    heads = lambda a, lead: a.reshape(1, *lead, N_HEADS, HEAD_DIM)
    return (y_prompt, y_sample,
            heads(k_f, (b, t)), heads(v_f, (b, t)), conv_tail[:, 6:8].reshape(1, b, 2, CONV_DIM),
            heads(k_s, (db, ts)), heads(v_s, (db, ts)),
            u_s.reshape(db, ts, CONV_DIM)[:, ts - 2:].reshape(1, db, 2, CONV_DIM))
```
